```python
import jax, jax.numpy as jnp
from jax import lax
import numpy as np

D_MODEL = 2048
BATCH = 8
SEQ = 2048
DEPTH = 4

CTX_LEN = 256
GRID_W = 64
N_MIXERS = 4
Q_BLOCK = 128
ROPE_THETA = 10000.0
NORM_EPS = 1e-6

RW_HEAD = 64
RW_HEADS = D_MODEL // RW_HEAD
RW_DECAY_LORA = 96
RW_ICLR_LORA = 96
RW_GATE_LORA = 256
RW_GN_EPS = 64e-5

MLA_HEADS = 16
MLA_Q_RANK = 512
MLA_KV_RANK = 512
MLA_NOPE = 128
MLA_ROPE = 64
MLA_V = 128

FN_GROUPS = 8
FN_GROUP_W = D_MODEL // FN_GROUPS

GQA_HEADS = 16
GQA_KV_HEADS = 4
GQA_HEAD = 128

N_EXPERTS = 32
TOP_K = 4
EXPERT_FF = 896
SWIGLU_LIMIT = 7.0
SWIGLU_ALPHA = 1.702
EXPERT_BLOCK = 128

kernel_name = 'hybrid_rwkv7_mla_fnet_gqa_moe_dit'


def rms_norm(x, gain, eps=NORM_EPS):
    xf = x.astype(jnp.float32)
    y = xf * lax.rsqrt(jnp.mean(xf * xf, axis=-1, keepdims=True) + eps)
    return (y * gain.astype(jnp.float32)).astype(x.dtype)


def modulate(u, shift, scale):
    return u * (1.0 + scale) + shift


def axial_rope_tables(pos_row, pos_col, rot_dim):
    quarter = rot_dim // 4
    inv = ROPE_THETA ** (-jnp.arange(quarter, dtype=jnp.float32) / quarter)
    ang = jnp.concatenate([pos_row[:, None].astype(jnp.float32) * inv,
                           pos_col[:, None].astype(jnp.float32) * inv], axis=-1)
    return jnp.cos(ang), jnp.sin(ang)


def apply_rope(x, cos, sin):
    half = x.shape[-1] // 2
    xf = x.astype(jnp.float32)
    x1, x2 = xf[..., :half], xf[..., half:]
    return jnp.concatenate([x1 * cos - x2 * sin, x2 * cos + x1 * sin], axis=-1).astype(x.dtype)


def block_attention(q, k, v):
    b, hk, g, sq, dq = q.shape
    nblk = sq // Q_BLOCK
    scale = dq ** -0.5
    qb = jnp.moveaxis(q.reshape(b, hk, g, nblk, Q_BLOCK, dq), 3, 0)

    def one_block(qblk):
        s = jnp.einsum('bhgqd,bhkd->bhgqk', qblk, k, preferred_element_type=jnp.float32) * scale
        p = jax.nn.softmax(s, axis=-1).astype(v.dtype)
        return jnp.einsum('bhgqk,bhkd->bhgqd', p, v)

    o = lax.map(one_block, qb)
    return jnp.moveaxis(o, 0, 3).reshape(b, hk, g, sq, v.shape[-1])


def centred_shift(u):
    up = jnp.pad(u, ((0, 0), (1, 1), (0, 0)))
    return 0.5 * (up[:, :-2] + up[:, 2:]) - u


def wkv7_scan(r, w, k, v, a, b, state, reverse, emit):
    xs = tuple(jnp.swapaxes(t.astype(jnp.float32), 0, 1) for t in (r, w, k, v, a, b))

    def step(s, inp):
        rt, wt, kt, vt, at, bt = inp
        sa = jnp.einsum('bhvk,bhk->bhv', s, at)
        s = s * wt[:, :, None, :] + sa[..., None] * bt[:, :, None, :] + vt[..., None] * kt[:, :, None, :]
        y = jnp.einsum('bhvk,bhk->bhv', s, rt) if emit else None
        return s, y

    state, ys = lax.scan(step, state, xs, reverse=reverse)
    return state, (jnp.swapaxes(ys, 0, 1) if emit else None)


def head_group_norm(y, gain, bias):
    b, t, h, n = y.shape
    mu = jnp.mean(y, axis=-1, keepdims=True)
    var = jnp.mean(jnp.square(y - mu), axis=-1, keepdims=True)
    yn = ((y - mu) * lax.rsqrt(var + RW_GN_EPS)).reshape(b, t, h * n)
    return yn * gain.astype(jnp.float32) + bias.astype(jnp.float32)


def rwkv7_mixer(uc, ux, mu, w_r, w_k, w_v, w_o, w0, w1, w2, a0, a1, a2, g1, g2, k_k, k_a, r_k,
                lnx_g, lnx_b, need_ctx):
    bsz, n_ctx, d = uc.shape
    u = jnp.concatenate([uc, ux], axis=1)
    t_all = u.shape[1]
    du = jnp.concatenate([centred_shift(uc), centred_shift(ux)], axis=1)
    xr, xw, xk, xv, xa, xg = (u + du * mu[n] for n in range(6))

    def heads(t):
        return t.reshape(bsz, t.shape[1], RW_HEADS, RW_HEAD)

    r, k, v = heads(xr @ w_r), heads(xk @ w_k), heads(xv @ w_v)
    kkf = (k * k_k.reshape(RW_HEADS, RW_HEAD)).astype(jnp.float32)
    kk = kkf / jnp.maximum(jnp.sqrt(jnp.sum(kkf * kkf, axis=-1, keepdims=True)), 1e-12)
    t0 = 0 if need_ctx else n_ctx
    acc = None
    for dr in range(2):
        wlog = -jax.nn.softplus(-(w0[dr] + jnp.tanh(xw @ w1[dr]) @ w2[dr])) - 0.5
        decay = heads(jnp.exp(-jnp.exp(wlog.astype(jnp.float32))))
        a = heads(jax.nn.sigmoid(a0[dr] + (xa @ a1[dr]) @ a2[dr]))
        kd = k * (1.0 + (a - 1.0) * k_a.reshape(RW_HEADS, RW_HEAD))
        seq = (r, decay, kd, v, -kk, kk * a)
        s0 = jnp.zeros((bsz, RW_HEADS, RW_HEAD, RW_HEAD), jnp.float32)
        s_ctx, y_ctx = wkv7_scan(*(t[:, :n_ctx] for t in seq), s0, dr == 1, need_ctx)
        _, y_lat = wkv7_scan(*(t[:, n_ctx:] for t in seq), s_ctx, dr == 1, True)
        y = jnp.concatenate([y_ctx, y_lat], axis=1) if need_ctx else y_lat
        bonus = jnp.sum(r[:, t0:] * kd[:, t0:] * r_k, axis=-1, keepdims=True) * v[:, t0:]
        o = head_group_norm(y, lnx_g, lnx_b).astype(u.dtype) + bonus.reshape(bsz, t_all - t0, d)
        g = jax.nn.sigmoid(xg[:, t0:] @ g1[dr]) @ g2[dr]
        acc = o * g if acc is None else acc + o * g
    out = acc @ w_o
    if need_ctx:
        return out[:, :n_ctx], out[:, n_ctx:]
    return None, out


def merge_heads(o):
    b, h, t, dv = o.shape
    return o.transpose(0, 2, 1, 3).reshape(b, t, h * dv)


def mla_mixer(uc, ux, cos, sin, w_down, qa_norm, w_qb, kv_norm, w_kvb, qn_nope, qn_rope, kn_nope, kn_rope,
              w_o, need_ctx):
    def project(u, rope, with_q):
        bsz, t, _ = u.shape
        down = u @ (w_down if with_q else w_down[:, MLA_Q_RANK:])
        c_kv = down[..., -(MLA_KV_RANK + MLA_ROPE):-MLA_ROPE]
        k_rope = rms_norm(down[..., -MLA_ROPE:], kn_rope)[:, None]
        kvb = (rms_norm(c_kv, kv_norm) @ w_kvb).reshape(bsz, t, MLA_HEADS, MLA_NOPE + MLA_V).transpose(0, 2, 1, 3)
        k_nope = rms_norm(kvb[..., :MLA_NOPE], kn_nope)
        val = kvb[..., MLA_NOPE:]
        if rope:
            k_rope = apply_rope(k_rope, cos, sin)
        key = jnp.concatenate([k_nope, jnp.broadcast_to(k_rope, (bsz, MLA_HEADS, t, MLA_ROPE))], axis=-1)
        q = None
        if with_q:
            q = (rms_norm(down[..., :MLA_Q_RANK], qa_norm) @ w_qb).reshape(
                bsz, t, MLA_HEADS, MLA_NOPE + MLA_ROPE).transpose(0, 2, 1, 3)
            q_nope = rms_norm(q[..., :MLA_NOPE], qn_nope)
            q_rope = rms_norm(q[..., MLA_NOPE:], qn_rope)
            if rope:
                q_rope = apply_rope(q_rope, cos, sin)
            q = jnp.concatenate([q_nope, q_rope], axis=-1)
        return q, key, val

    qc, kc, vc = project(uc, False, need_ctx)
    qx, kx, vx = project(ux, True, True)
    k_all = jnp.concatenate([kc, kx], axis=2)
    v_all = jnp.concatenate([vc, vx], axis=2)
    yx = merge_heads(block_attention(qx[:, :, None], k_all, v_all)[:, :, 0]) @ w_o
    yc = merge_heads(block_attention(qc[:, :, None], kc, vc)[:, :, 0]) @ w_o if need_ctx else None
    return yc, yx


def fourier_mixer(uc, ux, w, b, need_ctx):
    def mix(u):
        bsz, t, d = u.shape
        ug = u.astype(jnp.float32).reshape(bsz, t, FN_GROUPS, FN_GROUP_W)
        f = jnp.fft.fft2(ug, axes=(1, 3), norm='ortho').real
        return f.reshape(bsz, t, d).astype(u.dtype) @ w + b
    return (mix(uc) if need_ctx else None), mix(ux)


def gqa_mixer(uc, ux, cos, sin, w_qkv, q_norm, k_norm, w_o, need_ctx):
    n_q = GQA_HEADS * GQA_HEAD
    n_kv = GQA_KV_HEADS * GQA_HEAD
    grp = GQA_HEADS // GQA_KV_HEADS

    def project(u, rope, with_q):
        bsz, t, _ = u.shape
        qkv = u @ (w_qkv if with_q else w_qkv[:, n_q:])
        kv = qkv[..., -2 * n_kv:]
        k = rms_norm(kv[..., :n_kv].reshape(bsz, t, GQA_KV_HEADS, GQA_HEAD), k_norm).transpose(0, 2, 1, 3)
        v = kv[..., n_kv:].reshape(bsz, t, GQA_KV_HEADS, GQA_HEAD).transpose(0, 2, 1, 3)
        if rope:
            k = apply_rope(k, cos, sin)
        q = None
        if with_q:
            q = rms_norm(qkv[..., :n_q].reshape(bsz, t, GQA_KV_HEADS, grp, GQA_HEAD), q_norm).transpose(0, 2, 3, 1, 4)
            if rope:
                q = apply_rope(q, cos, sin)
        return q, k, v

    def out_proj(o):
        bsz, _, _, t, _ = o.shape
        return o.transpose(0, 3, 1, 2, 4).reshape(bsz, t, n_q) @ w_o

    qc, kc, vc = project(uc, False, need_ctx)
    qx, kx, vx = project(ux, True, True)
    yx = out_proj(block_attention(qx, jnp.concatenate([kc, kx], axis=2), jnp.concatenate([vc, vx], axis=2)))
    yc = out_proj(block_attention(qc, kc, vc)) if need_ctx else None
    return yc, yx


def moe_ffn(v, w_r, b_r, w_gate, b_gate, w_up, b_up, w_down, b_down):
    n_tok, d = v.shape
    logits = jnp.einsum('nd,de->ne', v, w_r, preferred_element_type=jnp.float32) + b_r.astype(jnp.float32)
    top_logit, top_idx = lax.top_k(logits, TOP_K)
    top_w = jax.nn.softmax(top_logit, axis=-1)
    n_slot = n_tok * TOP_K
    slot_e = top_idx.reshape(n_slot)
    slot_tok = jnp.repeat(jnp.arange(n_tok, dtype=jnp.int32), TOP_K)
    slot_w = top_w.reshape(n_slot)
    counts = jnp.bincount(slot_e, length=N_EXPERTS)
    padded = (counts + EXPERT_BLOCK - 1) // EXPERT_BLOCK * EXPERT_BLOCK
    pad_end = jnp.cumsum(padded)
    pad_start = pad_end - padded
    start = jnp.cumsum(counts) - counts
    order = jnp.argsort(slot_e)
    e_sorted = slot_e[order]
    dest = pad_start[e_sorted] + jnp.arange(n_slot) - start[e_sorted]
    n_rows = (n_slot + EXPERT_BLOCK - 1) // EXPERT_BLOCK * EXPERT_BLOCK + N_EXPERTS * EXPERT_BLOCK
    row_tok = jnp.full((n_rows,), n_tok, jnp.int32).at[dest].set(slot_tok[order])
    row_w = jnp.zeros((n_rows,), jnp.float32).at[dest].set(slot_w[order])
    n_blk = n_rows // EXPERT_BLOCK
    blk_e = jnp.minimum(jnp.searchsorted(pad_end, jnp.arange(n_blk) * EXPERT_BLOCK, side='right'), N_EXPERTS - 1)
    v_pad = jnp.concatenate([v, jnp.zeros((1, d), v.dtype)], axis=0)

    def expert_block(args):
        tok, e = args
        xb = v_pad[tok]
        gt = jnp.minimum(xb @ w_gate[e] + b_gate[e], SWIGLU_LIMIT)
        up = jnp.clip(xb @ w_up[e] + b_up[e], -SWIGLU_LIMIT, SWIGLU_LIMIT)
        return (gt * jax.nn.sigmoid(SWIGLU_ALPHA * gt) * (up + 1.0)) @ w_down[e] + b_down[e]

    y = lax.map(expert_block, (row_tok.reshape(n_blk, EXPERT_BLOCK), blk_e))
    y = y.reshape(n_rows, d) * row_w[:, None].astype(v.dtype)
    return jax.ops.segment_sum(y, row_tok, num_segments=n_tok + 1)[:n_tok]


def setup_inputs(seed: int = 0) -> dict:
    key = jax.random.key(seed)
    keys = iter(jax.random.split(key, 64))
    f32 = jnp.float32

    def nrm(shape, scale=1.0):
        return scale * jax.random.normal(next(keys), shape, f32)

    def gain(shape):
        return 1.0 + nrm(shape, 0.02)

    D = D_MODEL
    na, nb, nc, nd = (len(range(m, DEPTH, N_MIXERS)) for m in range(N_MIXERS))
    H, N = RW_HEADS, RW_HEAD
    inp = {}
    inp['x'] = nrm((BATCH, SEQ, D))
    inp['c'] = nrm((BATCH, D))
    inp['ctx'] = nrm((BATCH, CTX_LEN, D))
    inp['c_ctx'] = nrm((D,))
    inp['mod_w'] = nrm((DEPTH, D, 6 * D), 0.5 * D ** -0.5)
    inp['mod_b'] = nrm((DEPTH, 6 * D), 0.01)
    inp['norm_mix'] = gain((DEPTH, D))
    inp['norm_ffn'] = gain((DEPTH, D))
    inp['router_w'] = nrm((DEPTH, D, N_EXPERTS), D ** -0.5)
    inp['router_b'] = nrm((DEPTH, N_EXPERTS), 0.01)
    inp['exp_w_gate'] = nrm((DEPTH, N_EXPERTS, D, EXPERT_FF), D ** -0.5)
    inp['exp_b_gate'] = nrm((DEPTH, N_EXPERTS, EXPERT_FF), 0.01)
    inp['exp_w_up'] = nrm((DEPTH, N_EXPERTS, D, EXPERT_FF), D ** -0.5)
    inp['exp_b_up'] = nrm((DEPTH, N_EXPERTS, EXPERT_FF), 0.01)
    inp['exp_w_down'] = nrm((DEPTH, N_EXPERTS, EXPERT_FF, D), EXPERT_FF ** -0.5)
    inp['exp_b_down'] = nrm((DEPTH, N_EXPERTS, D), 0.01)
    inp['rw_mu'] = jax.random.uniform(next(keys), (na, 6, D), f32)
    inp['rw_wr'] = nrm((na, D, D), D ** -0.5)
    inp['rw_wk'] = nrm((na, D, D), D ** -0.5)
    inp['rw_wv'] = nrm((na, D, D), D ** -0.5)
    inp['rw_wo'] = nrm((na, D, D), D ** -0.5)
    inp['rw_w0'] = -1.0 + nrm((na, 2, D), 0.5)
    inp['rw_w1'] = nrm((na, 2, D, RW_DECAY_LORA), D ** -0.5)
    inp['rw_w2'] = nrm((na, 2, RW_DECAY_LORA, D), 0.1 * RW_DECAY_LORA ** -0.5)
    inp['rw_a0'] = nrm((na, 2, D), 0.1)
    inp['rw_a1'] = nrm((na, 2, D, RW_ICLR_LORA), D ** -0.5)
    inp['rw_a2'] = nrm((na, 2, RW_ICLR_LORA, D), 0.1 * RW_ICLR_LORA ** -0.5)
    inp['rw_g1'] = nrm((na, 2, D, RW_GATE_LORA), D ** -0.5)
    inp['rw_g2'] = nrm((na, 2, RW_GATE_LORA, D), RW_GATE_LORA ** -0.5)
    inp['rw_kk'] = 0.85 + nrm((na, D), 0.02)
    inp['rw_ka'] = gain((na, D))
    inp['rw_rk'] = nrm((na, H, N), 0.1)
    inp['rw_lnx_g'] = gain((na, D))
    inp['rw_lnx_b'] = nrm((na, D), 0.01)
    inp['mla_w_down'] = nrm((nb, D, MLA_Q_RANK + MLA_KV_RANK + MLA_ROPE), D ** -0.5)
    inp['mla_qa_norm'] = gain((nb, MLA_Q_RANK))
    inp['mla_w_qb'] = nrm((nb, MLA_Q_RANK, MLA_HEADS * (MLA_NOPE + MLA_ROPE)), MLA_Q_RANK ** -0.5)
    inp['mla_kv_norm'] = gain((nb, MLA_KV_RANK))
    inp['mla_w_kvb'] = nrm((nb, MLA_KV_RANK, MLA_HEADS * (MLA_NOPE + MLA_V)), MLA_KV_RANK ** -0.5)
    inp['mla_qn_nope'] = gain((nb, MLA_NOPE))
    inp['mla_qn_rope'] = gain((nb, MLA_ROPE))
    inp['mla_kn_nope'] = gain((nb, MLA_NOPE))
    inp['mla_kn_rope'] = gain((nb, MLA_ROPE))
    inp['mla_wo'] = nrm((nb, MLA_HEADS * MLA_V, D), (MLA_HEADS * MLA_V) ** -0.5)
    inp['fn_w'] = nrm((nc, D, D), D ** -0.5)
    inp['fn_b'] = nrm((nc, D), 0.01)
    inp['gqa_w_qkv'] = nrm((nd, D, (GQA_HEADS + 2 * GQA_KV_HEADS) * GQA_HEAD), D ** -0.5)
    inp['gqa_q_norm'] = gain((nd, GQA_HEAD))
    inp['gqa_k_norm'] = gain((nd, GQA_HEAD))
    inp['gqa_wo'] = nrm((nd, GQA_HEADS * GQA_HEAD, D), (GQA_HEADS * GQA_HEAD) ** -0.5)
    return inp


def reference(x, c, ctx, c_ctx, mod_w, mod_b, norm_mix, norm_ffn, router_w, router_b, exp_w_gate, exp_b_gate,
              exp_w_up, exp_b_up, exp_w_down, exp_b_down, rw_mu, rw_wr, rw_wk, rw_wv, rw_wo, rw_w0, rw_w1, rw_w2,
              rw_a0, rw_a1, rw_a2, rw_g1, rw_g2, rw_kk, rw_ka, rw_rk, rw_lnx_g, rw_lnx_b, mla_w_down, mla_qa_norm,
              mla_w_qb, mla_kv_norm, mla_w_kvb, mla_qn_nope, mla_qn_rope, mla_kn_nope, mla_kn_rope, mla_wo,
              fn_w, fn_b, gqa_w_qkv, gqa_q_norm, gqa_k_norm, gqa_wo):
    n_lat = x.shape[1]
    rows = n_lat // GRID_W
    pos_row = jnp.repeat(jnp.arange(rows), GRID_W)
    pos_col = jnp.tile(jnp.arange(GRID_W), rows)
    cos_mla, sin_mla = axial_rope_tables(pos_row, pos_col, MLA_ROPE)
    cos_gqa, sin_gqa = axial_rope_tables(pos_row, pos_col, GQA_HEAD)
    silu_c = jax.nn.silu(c)
    silu_cc = jax.nn.silu(c_ctx)
    hx, hc = x, ctx
    for i in range(DEPTH):
        m, j = i % N_MIXERS, i // N_MIXERS
        need_ctx = i < DEPTH - 1
        mod_x = jnp.split((silu_c @ mod_w[i] + mod_b[i])[:, None, :], 6, axis=-1)
        mod_c = jnp.split(silu_cc @ mod_w[i] + mod_b[i], 6, axis=-1)
        ux = modulate(rms_norm(hx, norm_mix[i]), mod_x[0], mod_x[1])
        uc = modulate(rms_norm(hc, norm_mix[i]), mod_c[0], mod_c[1])
        if m == 0:
            yc, yx = rwkv7_mixer(uc, ux, rw_mu[j], rw_wr[j], rw_wk[j], rw_wv[j], rw_wo[j], rw_w0[j], rw_w1[j],
                                 rw_w2[j], rw_a0[j], rw_a1[j], rw_a2[j], rw_g1[j], rw_g2[j], rw_kk[j], rw_ka[j],
                                 rw_rk[j], rw_lnx_g[j], rw_lnx_b[j], need_ctx)
        elif m == 1:
            yc, yx = mla_mixer(uc, ux, cos_mla, sin_mla, mla_w_down[j], mla_qa_norm[j], mla_w_qb[j],
                               mla_kv_norm[j], mla_w_kvb[j], mla_qn_nope[j], mla_qn_rope[j], mla_kn_nope[j],
                               mla_kn_rope[j], mla_wo[j], need_ctx)
        elif m == 2:
            yc, yx = fourier_mixer(uc, ux, fn_w[j], fn_b[j], need_ctx)
        else:
            yc, yx = gqa_mixer(uc, ux, cos_gqa, sin_gqa, gqa_w_qkv[j], gqa_q_norm[j], gqa_k_norm[j],
                               gqa_wo[j], need_ctx)
        hx = hx + mod_x[2] * yx
        vx = modulate(rms_norm(hx, norm_ffn[i]), mod_x[3], mod_x[4])
        moe_args = (router_w[i], router_b[i], exp_w_gate[i], exp_b_gate[i], exp_w_up[i], exp_b_up[i],
                    exp_w_down[i], exp_b_down[i])
        if need_ctx:
            hc = hc + mod_c[2] * yc
            vc = modulate(rms_norm(hc, norm_ffn[i]), mod_c[3], mod_c[4])
            n_c = vc.shape[0] * vc.shape[1]
            f = moe_ffn(jnp.concatenate([vc.reshape(n_c, -1), vx.reshape(-1, vx.shape[-1])], axis=0), *moe_args)
            hc = hc + mod_c[5] * f[:n_c].reshape(vc.shape)
            hx = hx + mod_x[5] * f[n_c:].reshape(vx.shape)
        else:
            hx = hx + mod_x[5] * moe_ffn(vx.reshape(-1, vx.shape[-1]), *moe_args).reshape(vx.shape)
    return hx
```

```python
import functools
import math

import numpy as np
import jax
import jax.numpy as jnp
from jax import lax
from jax.experimental import pallas as pl
from jax.experimental.pallas import tpu as pltpu

F32 = jnp.float32
BF16 = jnp.bfloat16

LANES = 128
SUBLANES = 8
VMEM_LIMIT_BYTES = 56 * 1024 * 1024

GRID_W = 64
ROPE_THETA = 10000.0
NORM_EPS = 1e-6
RW_HEAD = 64
RW_GN_EPS = 64e-5
MLA_HEADS = 16
MLA_Q_RANK = 512
MLA_KV_RANK = 512
MLA_NOPE = 128
MLA_ROPE = 64
MLA_V = 128
FN_GROUPS = 8
GQA_HEADS = 16
GQA_KV_HEADS = 4
GQA_HEAD = 128
TOP_K = 4
SWIGLU_LIMIT = 7.0
SWIGLU_ALPHA = 1.702

TM = 256
WKV_CHUNK = 64
EXPERT_ROWS = 256
PAIR = 2 * RW_HEAD


def _cparams(sem):
    return pltpu.CompilerParams(dimension_semantics=sem, vmem_limit_bytes=VMEM_LIMIT_BYTES)


def _dot(a, b):
    return jnp.dot(a.astype(BF16), b.astype(BF16), preferred_element_type=F32)


def _dot_nt(a, b):
    return lax.dot_general(a.astype(BF16), b.astype(BF16), (((1,), (1,)), ((), ())),
                           preferred_element_type=F32)


def _rms(x, gain):
    return x * lax.rsqrt(jnp.mean(x * x, axis=-1, keepdims=True) + NORM_EPS) * gain


def _sigmoid(x):
    return 1.0 / (1.0 + jnp.exp(-x))


class Layout:
    def __init__(self, batch, ctx_len, seq, d):
        self.b, self.ctx, self.seq, self.d = batch, ctx_len, seq, d
        self.tt = ctx_len + seq
        self.n = batch * self.tt
        assert ctx_len % TM == 0 and seq % TM == 0
        self.nb = self.tt // TM
        self.cb = ctx_len // TM
        self.nblocks = self.n // TM

    def seg(self, i):
        return jnp.where(i % self.nb < self.cb, self.b, i // self.nb)


def _rows(ncols, tm=TM):
    return pl.BlockSpec((tm, ncols), lambda i: (i, 0))


def _full(shape):
    nd = len(shape)
    return pl.BlockSpec(shape, lambda i: (0,) * nd)


def _modspec(lay):
    return pl.BlockSpec((1, 6, lay.d), lambda i: (lay.seg(i), 0, 0))


def _posrows(lay, ncols=LANES):
    return pl.BlockSpec((TM, ncols), lambda i: (i % lay.nb, 0))


def _pairs(tm=TM):
    return pl.BlockSpec((None, tm, PAIR), lambda i: (0, i, 0))


def _mod_body(c_ref, w_ref, b_ref, o_ref):
    c = c_ref[...]
    s = c * _sigmoid(c)
    o_ref[0] = _dot(s, w_ref[0]) + b_ref[0]


def modulation(cin, mod_w, mod_b):
    depth, d, n6 = mod_w.shape
    tn = 1024
    return pl.pallas_call(
        _mod_body,
        grid=(depth, n6 // tn),
        in_specs=[pl.BlockSpec((16, d), lambda l, j: (0, 0)),
                  pl.BlockSpec((1, d, tn), lambda l, j: (l, 0, j)),
                  pl.BlockSpec((1, 1, tn), lambda l, j: (l, 0, j))],
        out_specs=pl.BlockSpec((1, 16, tn), lambda l, j: (l, 0, j)),
        out_shape=jax.ShapeDtypeStruct((depth, 16, n6), F32),
        compiler_params=_cparams(("parallel", "parallel")),
        name="modulation",
    )(cin, mod_w, mod_b.reshape(depth, 1, n6))


def _rw_prep_body(lay, h_ref, hp_ref, hn_ref, g_ref, mod_ref, mu_ref, *outs):
    j = pl.program_id(0) % lay.nb
    seg_start = jnp.logical_or(j == 0, j == lay.cb)
    seg_end = jnp.logical_or(j == lay.cb - 1, j == lay.nb - 1)
    gain = g_ref[...]
    shift, scale = mod_ref[0, 0:1, :], mod_ref[0, 1:2, :]

    def norm_mod(x):
        return _rms(x, gain) * (1.0 + scale) + shift

    u = norm_mod(h_ref[...])
    prev = jnp.where(seg_start, 0.0, norm_mod(hp_ref[SUBLANES - 1:SUBLANES, :]))
    nxt = jnp.where(seg_end, 0.0, norm_mod(hn_ref[0:1, :]))
    row = lax.broadcasted_iota(jnp.int32, u.shape, 0)
    up = jnp.where(row == 0, prev, pltpu.roll(u, 1, 0))
    un = jnp.where(row == TM - 1, nxt, pltpu.roll(u, TM - 1, 0))
    du = 0.5 * (up + un) - u
    for n, o_ref in enumerate(outs):
        o_ref[...] = (u + du * mu_ref[n:n + 1, :]).astype(BF16)


def rw_prep(lay, h, gain, mod, mu):
    d = lay.d
    r8 = TM // SUBLANES
    last8 = lay.n // SUBLANES - 1
    mu8 = jnp.concatenate([mu, jnp.zeros((2, d), F32)], axis=0)
    return pl.pallas_call(
        functools.partial(_rw_prep_body, lay),
        grid=(lay.nblocks,),
        in_specs=[_rows(d),
                  pl.BlockSpec((SUBLANES, d), lambda i: (jnp.maximum(i * r8 - 1, 0), 0)),
                  pl.BlockSpec((SUBLANES, d), lambda i: (jnp.minimum((i + 1) * r8, last8), 0)),
                  _full((1, d)), _modspec(lay), _full((8, d))],
        out_specs=[_rows(d)] * 6,
        out_shape=[jax.ShapeDtypeStruct((lay.n, d), BF16)] * 6,
        compiler_params=_cparams(("parallel",)),
        name="rw_prep",
    )(h, h, h, gain.reshape(1, d), mod, mu8)


def _linear_pairs_body(npairs, x_ref, w_ref, o_ref):
    y = _dot(x_ref[...], w_ref[...])
    for p in range(npairs):
        o_ref[p] = y[:, p * PAIR:(p + 1) * PAIR]


def linear_pairs(lay, x, w):
    d = lay.d
    npairs = d // PAIR
    return pl.pallas_call(
        functools.partial(_linear_pairs_body, npairs),
        grid=(lay.nblocks,),
        in_specs=[_rows(d), _full((d, d))],
        out_specs=pl.BlockSpec((npairs, TM, PAIR), lambda i: (0, i, 0)),
        out_shape=jax.ShapeDtypeStruct((npairs, lay.n, PAIR), F32),
        compiler_params=_cparams(("parallel",)),
        name="rw_linear",
    )(x, w)


def _rw_lora_body(npairs, xw_ref, xa_ref, xg_ref, w1_ref, w2_ref, a1_ref, a2_ref, g1_ref, g2_ref,
                  w0_ref, a0_ref, lw_ref, ag_ref, g_ref):
    hw = jnp.tanh(_dot(xw_ref[...], w1_ref[...]))
    ha = _dot(xa_ref[...], a1_ref[...])
    hg = _sigmoid(_dot(xg_ref[...], g1_ref[...]))
    rw = w1_ref.shape[1] // 2
    rg = g1_ref.shape[1] // 2
    for dr in range(2):
        z = w0_ref[dr:dr + 1, :] + _dot(hw[:, dr * rw:(dr + 1) * rw], w2_ref[dr])
        lw = -math.exp(-0.5) * _sigmoid(z)
        a = _sigmoid(a0_ref[dr:dr + 1, :] + _dot(ha[:, dr * rw:(dr + 1) * rw], a2_ref[dr]))
        g = _dot(hg[:, dr * rg:(dr + 1) * rg], g2_ref[dr])
        for p in range(npairs):
            sl = slice(p * PAIR, (p + 1) * PAIR)
            lw_ref[dr, p] = lw[:, sl]
            ag_ref[dr, p] = a[:, sl]
            g_ref[dr, p] = g[:, sl]


def _pad_lora(w1, w2):
    r = w1.shape[2]
    rp = -(-r // LANES) * LANES
    w1p = jnp.pad(w1, ((0, 0), (0, 0), (0, rp - r)))
    w1p = jnp.concatenate([w1p[0], w1p[1]], axis=1).astype(BF16)
    w2p = jnp.pad(w2, ((0, 0), (0, rp - r), (0, 0))).astype(BF16)
    return w1p, w2p


def rw_lora(lay, xw, xa, xg, w0, w1, w2, a0, a1, a2, g1, g2):
    d = lay.d
    npairs = d // PAIR
    w1p, w2p = _pad_lora(w1, w2)
    a1p, a2p = _pad_lora(a1, a2)
    g1p, g2p = _pad_lora(g1, g2)
    ospec = pl.BlockSpec((2, npairs, TM, PAIR), lambda i: (0, 0, i, 0))
    oshape = jax.ShapeDtypeStruct((2, npairs, lay.n, PAIR), F32)
    return pl.pallas_call(
        functools.partial(_rw_lora_body, npairs),
        grid=(lay.nblocks,),
        in_specs=[_rows(d)] * 3 + [_full(w1p.shape), _full(w2p.shape), _full(a1p.shape), _full(a2p.shape),
                                    _full(g1p.shape), _full(g2p.shape), _full((2, d)), _full((2, d))],
        out_specs=[ospec] * 3,
        out_shape=[oshape] * 3,
        compiler_params=_cparams(("parallel",)),
        name="rw_lora",
    )(xw, xa, xg, w1p, w2p, a1p, a2p, g1p, g2p, w0, a0)


def _bd(z, head0):
    return jnp.concatenate([jnp.where(head0, z, 0.0), jnp.where(head0, 0.0, z)], axis=0)


def _head_sum(x, head0):
    s0 = jnp.sum(jnp.where(head0, x, 0.0), axis=-1, keepdims=True)
    s1 = jnp.sum(jnp.where(head0, 0.0, x), axis=-1, keepdims=True)
    return jnp.where(head0, s0, s1)


def _wkv_pair_chunk(rev, r, k, v, lw, ag, g, kk_p, ka_p, rk_p, lg_p, lb_p, ht):
    c = WKV_CHUNK
    t = lax.broadcasted_iota(jnp.int32, (c, PAIR), 0)
    lane = lax.broadcasted_iota(jnp.int32, (c, PAIR), 1)
    s = lane % RW_HEAD
    head0 = lane < RW_HEAD
    strict = (s > t) if rev else (s < t)
    incl = (s >= t) if rev else (s <= t)

    kkf = k * kk_p
    kk = kkf / jnp.maximum(jnp.sqrt(_head_sum(kkf * kkf, head0)), 1e-12)
    kd = k * (1.0 + (ag - 1.0) * ka_p)
    avec = -kk
    bvec = kk * ag

    cs = lw
    for sh in (1, 2, 4, 8, 16, 32):
        if rev:
            cs = cs + jnp.where(t + sh < c, pltpu.roll(cs, c - sh, 0), 0.0)
        else:
            cs = cs + jnp.where(t >= sh, pltpu.roll(cs, sh, 0), 0.0)
    cl = cs[0:1, :] if rev else cs[c - 1:c, :]
    e_neg = jnp.exp(-cs)
    e_end = jnp.exp(cl - cs)
    at = avec * jnp.exp(cs - lw)
    rt = r * jnp.exp(cs)
    bt = bvec * e_neg
    kt = kd * e_neg

    ar = jnp.concatenate([at, rt], axis=0)
    p = _dot_nt(ar, jnp.concatenate([_bd(bt, head0), _bd(kt, head0)], axis=0))
    l_ab = jnp.where(strict, p[0:c, 0:PAIR], 0.0)
    l_ak = jnp.where(strict, p[0:c, PAIR:2 * PAIR], 0.0)
    m_rb = jnp.where(incl, p[c:2 * c, 0:PAIR], 0.0)
    m_rk = jnp.where(incl, p[c:2 * c, PAIR:2 * PAIR], 0.0)
    arh = _dot_nt(ar, ht)
    vbd = _bd(v, head0)
    x = arh[0:c] + _dot(l_ak, vbd)
    tinv = jnp.where(s == t, 1.0, 0.0) + l_ab
    pw = l_ab
    for _ in range(5):
        pw = _dot(pw, _bd(pw, head0))
        tinv = tinv + _dot(tinv, _bd(pw, head0))
    u = _dot(tinv, _bd(x, head0))
    y = arh[c:2 * c] + _dot(jnp.concatenate([m_rb, m_rk], axis=1),
                            jnp.concatenate([_bd(u, head0), vbd], axis=0))
    uv_t = jnp.concatenate([u, v], axis=0).T
    upd = _dot(uv_t, jnp.concatenate([bvec * e_end, kd * e_end], axis=0))
    rr = lax.broadcasted_iota(jnp.int32, (PAIR, PAIR), 0) // RW_HEAD
    cc = lax.broadcasted_iota(jnp.int32, (PAIR, PAIR), 1) // RW_HEAD
    ht_new = ht * jnp.exp(cl) + jnp.where(rr == cc, upd, 0.0)

    inv_n = 1.0 / RW_HEAD
    mu = _head_sum(y, head0) * inv_n
    yc = y - mu
    var = _head_sum(yc * yc, head0) * inv_n
    bonus = _head_sum(r * kd * rk_p, head0) * v
    o = yc * lax.rsqrt(var + RW_GN_EPS) * lg_p + lb_p + bonus
    return o * g, ht_new


def _wkv_body(rev, npairs, r_ref, k_ref, v_ref, lw_ref, ag_ref, g_ref, kk_ref, ka_ref, rk_ref, lg_ref, lb_ref,
              o_ref, ht_ref):
    @pl.when(pl.program_id(1) == 0)
    def _():
        ht_ref[...] = jnp.zeros_like(ht_ref)

    def pair(p, carry):
        og, ht = _wkv_pair_chunk(rev, r_ref[p], k_ref[p], v_ref[p], lw_ref[0, p], ag_ref[0, p], g_ref[0, p],
                                 kk_ref[p], ka_ref[p], rk_ref[p], lg_ref[p], lb_ref[p], ht_ref[p])
        o_ref[p] = og
        ht_ref[p] = ht
        return carry

    lax.fori_loop(0, npairs, pair, 0)


def wkv(lay, rev, r, k, v, lw, ag, g, params):
    npairs = lay.d // PAIR
    c = WKV_CHUNK
    nch = lay.tt // c
    cch = lay.ctx // c
    dr = 1 if rev else 0

    def rowblk(b, ci):
        if rev:
            return b * nch + jnp.where(ci < cch, cch - 1 - ci, nch + cch - 1 - ci)
        return b * nch + ci

    tok = pl.BlockSpec((npairs, c, PAIR), lambda b, ci: (0, rowblk(b, ci), 0))
    tokd = pl.BlockSpec((1, npairs, c, PAIR), lambda b, ci: (dr, 0, rowblk(b, ci), 0))
    par = pl.BlockSpec((npairs, 1, PAIR), lambda b, ci: (0, 0, 0))
    return pl.pallas_call(
        functools.partial(_wkv_body, rev, npairs),
        grid=(lay.b, nch),
        in_specs=[tok, tok, tok, tokd, tokd, tokd] + [par] * 5,
        out_specs=tok,
        out_shape=jax.ShapeDtypeStruct((npairs, lay.n, PAIR), F32),
        scratch_shapes=[pltpu.VMEM((npairs, PAIR, PAIR), F32)],
        compiler_params=_cparams(("parallel", "arbitrary")),
        name="wkv_rev" if rev else "wkv_fwd",
    )(r, k, v, lw, ag, g, *params)


def _rw_out_body(npairs, o0_ref, o1_ref, h_ref, mod_ref, w_ref, out_ref):
    acc = jnp.concatenate([o0_ref[p] + o1_ref[p] for p in range(npairs)], axis=-1)
    out_ref[...] = h_ref[...] + mod_ref[0, 2:3, :] * _dot(acc, w_ref[...])


def rw_out(lay, o0, o1, h, mod, w):
    d = lay.d
    npairs = d // PAIR
    pm = pl.BlockSpec((npairs, TM, PAIR), lambda i: (0, i, 0))
    return pl.pallas_call(
        functools.partial(_rw_out_body, npairs),
        grid=(lay.nblocks,),
        in_specs=[pm, pm, _rows(d), _modspec(lay), _full((d, d))],
        out_specs=_rows(d),
        out_shape=jax.ShapeDtypeStruct((lay.n, d), F32),
        compiler_params=_cparams(("parallel",)),
        name="rw_out",
    )(o0, o1, h, mod, w)


def rwkv_layer(lay, h, mod, gain, mu, w_r, w_k, w_v, w_o, w0, w1, w2, a0, a1, a2, g1, g2, k_k, k_a, r_k,
               lnx_g, lnx_b):
    npairs = lay.d // PAIR
    xr, xw, xk, xv, xa, xg = rw_prep(lay, h, gain, mod, mu)
    r = linear_pairs(lay, xr, w_r.astype(BF16))
    k = linear_pairs(lay, xk, w_k.astype(BF16))
    v = linear_pairs(lay, xv, w_v.astype(BF16))
    lw, ag, g = rw_lora(lay, xw, xa, xg, w0, w1, w2, a0, a1, a2, g1, g2)
    params = [t.reshape(npairs, 1, PAIR) for t in (k_k, k_a, r_k.reshape(-1), lnx_g, lnx_b)]
    o0 = wkv(lay, False, r, k, v, lw, ag, g, params)
    o1 = wkv(lay, True, r, k, v, lw, ag, g, params)
    return rw_out(lay, o0, o1, h, mod, w_o.astype(BF16))


def rope_tables(lay, rot_dim):
    quarter = rot_dim // 4
    half = rot_dim // 2
    t = np.arange(lay.seq)
    inv = ROPE_THETA ** (-np.arange(quarter, dtype=np.float32) / quarter)
    ang = np.concatenate([(t // GRID_W)[:, None].astype(np.float32) * inv,
                          (t % GRID_W)[:, None].astype(np.float32) * inv], axis=-1)
    ang = jnp.asarray(ang, F32)
    cos, sin = jnp.cos(ang), jnp.sin(ang)
    pad = LANES - rot_dim
    zer = jnp.zeros((lay.seq, half), F32)
    c = jnp.concatenate([cos, cos, jnp.ones((lay.seq, pad), F32)], axis=-1)
    s_lo = jnp.concatenate([-sin, zer, jnp.zeros((lay.seq, pad), F32)], axis=-1)
    s_hi = jnp.concatenate([zer, sin, jnp.zeros((lay.seq, pad), F32)], axis=-1)

    def with_ctx(tab, fill):
        return jnp.concatenate([jnp.full((lay.ctx, LANES), fill, F32), tab], axis=0)

    return with_ctx(c, 1.0), with_ctx(s_lo, 0.0), with_ctx(s_hi, 0.0)


def _rope(x, c, s_lo, s_hi, half):
    return x * c + pltpu.roll(x, LANES - half, 1) * s_lo + pltpu.roll(x, half, 1) * s_hi


def _attn_body(ctx, tq, q_ref, k_ref, v_ref, o_ref):
    s = _dot_nt(q_ref[...], k_ref[...])
    qrow = pl.program_id(2) * tq + lax.broadcasted_iota(jnp.int32, s.shape, 0)
    kcol = lax.broadcasted_iota(jnp.int32, s.shape, 1)
    s = jnp.where(jnp.logical_or(qrow >= ctx, kcol < ctx), s, -1e30)
    m = jnp.max(s, axis=-1, keepdims=True)
    p = jnp.exp(s - m)
    l = jnp.sum(p, axis=-1, keepdims=True)
    o_ref[...] = (_dot(p, v_ref[...]) / l).astype(o_ref.dtype)


def attention(lay, q, k, v, n_heads, kv_group, dq, dv, tq=768):
    nq = lay.tt // tq
    k3 = k.reshape(lay.b, lay.tt, k.shape[1])
    v3 = v.reshape(lay.b, lay.tt, v.shape[1])
    return pl.pallas_call(
        functools.partial(_attn_body, lay.ctx, tq),
        grid=(lay.b, n_heads, nq),
        in_specs=[pl.BlockSpec((tq, dq), lambda b, h, i: (b * nq + i, h)),
                  pl.BlockSpec((None, lay.tt, dq), lambda b, h, i: (b, 0, h // kv_group)),
                  pl.BlockSpec((None, lay.tt, dv), lambda b, h, i: (b, 0, h // kv_group))],
        out_specs=pl.BlockSpec((tq, dv), lambda b, h, i: (b * nq + i, h)),
        out_shape=jax.ShapeDtypeStruct((lay.n, n_heads * dv), BF16),
        compiler_params=_cparams(("parallel", "parallel", "arbitrary")),
        name="attention",
    )(q, k3, v3)


def _out_proj_body(x_ref, w_ref, b_ref, h_ref, mod_ref, out_ref):
    y = _dot(x_ref[...], w_ref[...]) + b_ref[...]
    out_ref[...] = h_ref[...] + mod_ref[0, 2:3, :] * y


def out_proj(lay, x, w, bias, h, mod):
    d = lay.d
    kdim = x.shape[1]
    return pl.pallas_call(
        _out_proj_body,
        grid=(lay.nblocks,),
        in_specs=[_rows(kdim), _full((kdim, d)), _full((1, d)), _rows(d), _modspec(lay)],
        out_specs=_rows(d),
        out_shape=jax.ShapeDtypeStruct((lay.n, d), F32),
        compiler_params=_cparams(("parallel",)),
        name="out_proj",
    )(x, w.astype(BF16), bias.reshape(1, d), h, mod)


def _mla_down_body(h_ref, g_ref, mod_ref, w_ref, qan_ref, kvn_ref, krn_ref, c_ref, slo_ref, shi_ref,
                   q_ref, ckv_ref, kr_ref):
    u = _rms(h_ref[...], g_ref[...]) * (1.0 + mod_ref[0, 1:2, :]) + mod_ref[0, 0:1, :]
    down = _dot(u, w_ref[...])
    q_ref[...] = _rms(down[:, 0:MLA_Q_RANK], qan_ref[...]).astype(BF16)
    ckv_ref[...] = _rms(down[:, MLA_Q_RANK:MLA_Q_RANK + MLA_KV_RANK], kvn_ref[...]).astype(BF16)
    kr = down[:, MLA_Q_RANK + MLA_KV_RANK:]
    ms = jnp.sum(kr * kr, axis=-1, keepdims=True) * (1.0 / MLA_ROPE)
    kr = kr * lax.rsqrt(ms + NORM_EPS) * krn_ref[...]
    kr_ref[...] = _rope(kr, c_ref[...], slo_ref[...], shi_ref[...], MLA_ROPE // 2).astype(BF16)


def _mla_q_body(scale, x_ref, w_ref, nn_ref, rn_ref, c_ref, slo_ref, shi_ref, q_ref):
    q = _dot(x_ref[...], w_ref[...])
    hw = 2 * LANES
    for hd in range(MLA_HEADS):
        nope = q[:, hd * hw:hd * hw + LANES]
        rope = q[:, hd * hw + LANES:(hd + 1) * hw]
        nope = _rms(nope, nn_ref[...])
        ms = jnp.sum(rope * rope, axis=-1, keepdims=True) * (1.0 / MLA_ROPE)
        rope = rope * lax.rsqrt(ms + NORM_EPS) * rn_ref[...]
        rope = _rope(rope, c_ref[...], slo_ref[...], shi_ref[...], MLA_ROPE // 2)
        q_ref[:, hd * hw:hd * hw + LANES] = (nope * scale).astype(BF16)
        q_ref[:, hd * hw + LANES:(hd + 1) * hw] = (rope * scale).astype(BF16)


def _mla_kv_body(x_ref, w_ref, kn_ref, kr_ref, k_ref, v_ref):
    kvb = _dot(x_ref[...], w_ref[...])
    hw = 2 * LANES
    kr = kr_ref[...]
    for hd in range(MLA_HEADS):
        k_ref[:, hd * hw:hd * hw + LANES] = _rms(kvb[:, hd * LANES:(hd + 1) * LANES], kn_ref[...]).astype(BF16)
        k_ref[:, hd * hw + LANES:(hd + 1) * hw] = kr
    v_ref[...] = kvb[:, MLA_HEADS * LANES:].astype(BF16)


def mla_layer(lay, h, mod, gain, w_down, qa_norm, w_qb, kv_norm, w_kvb, qn_nope, qn_rope, kn_nope, kn_rope, w_o):
    d = lay.d
    tabs = rope_tables(lay, MLA_ROPE)
    pad64 = lambda t: jnp.concatenate([t, jnp.zeros((LANES - MLA_ROPE,), F32)]).reshape(1, LANES)
    wd = jnp.pad(w_down, ((0, 0), (0, LANES - MLA_ROPE))).astype(BF16)
    nd = wd.shape[1]
    q_lat, ckv, kr = pl.pallas_call(
        _mla_down_body,
        grid=(lay.nblocks,),
        in_specs=[_rows(d), _full((1, d)), _modspec(lay), _full((d, nd)), _full((1, MLA_Q_RANK)),
                  _full((1, MLA_KV_RANK)), _full((1, LANES))] + [_posrows(lay)] * 3,
        out_specs=[_rows(MLA_Q_RANK), _rows(MLA_KV_RANK), _rows(LANES)],
        out_shape=[jax.ShapeDtypeStruct((lay.n, MLA_Q_RANK), BF16), jax.ShapeDtypeStruct((lay.n, MLA_KV_RANK), BF16),
                   jax.ShapeDtypeStruct((lay.n, LANES), BF16)],
        compiler_params=_cparams(("parallel",)),
        name="mla_down",
    )(h, gain.reshape(1, d), mod, wd, qa_norm.reshape(1, -1), kv_norm.reshape(1, -1), pad64(kn_rope), *tabs)

    hq = MLA_NOPE + MLA_ROPE
    wq = w_qb.reshape(MLA_Q_RANK, MLA_HEADS, hq)
    wq = jnp.pad(wq, ((0, 0), (0, 0), (0, 2 * LANES - hq))).reshape(MLA_Q_RANK, MLA_HEADS * 2 * LANES).astype(BF16)
    dq = 2 * LANES
    scale = float(hq) ** -0.5
    q = pl.pallas_call(
        functools.partial(_mla_q_body, scale),
        grid=(lay.nblocks,),
        in_specs=[_rows(MLA_Q_RANK), _full(wq.shape), _full((1, LANES)), _full((1, LANES))] + [_posrows(lay)] * 3,
        out_specs=_rows(MLA_HEADS * dq),
        out_shape=jax.ShapeDtypeStruct((lay.n, MLA_HEADS * dq), BF16),
        compiler_params=_cparams(("parallel",)),
        name="mla_q",
    )(q_lat, wq, qn_nope.reshape(1, LANES), pad64(qn_rope), *tabs)

    wkv_ = w_kvb.reshape(MLA_KV_RANK, MLA_HEADS, MLA_NOPE + MLA_V)
    wkv_ = jnp.concatenate([wkv_[:, :, :MLA_NOPE].reshape(MLA_KV_RANK, -1),
                            wkv_[:, :, MLA_NOPE:].reshape(MLA_KV_RANK, -1)], axis=1).astype(BF16)
    k, v = pl.pallas_call(
        _mla_kv_body,
        grid=(lay.nblocks,),
        in_specs=[_rows(MLA_KV_RANK), _full(wkv_.shape), _full((1, LANES)), _rows(LANES)],
        out_specs=[_rows(MLA_HEADS * dq), _rows(MLA_HEADS * MLA_V)],
        out_shape=[jax.ShapeDtypeStruct((lay.n, MLA_HEADS * dq), BF16),
                   jax.ShapeDtypeStruct((lay.n, MLA_HEADS * MLA_V), BF16)],
        compiler_params=_cparams(("parallel",)),
        name="mla_kv",
    )(ckv, wkv_, kn_nope.reshape(1, LANES), kr)

    o = attention(lay, q, k, v, MLA_HEADS, 1, dq, MLA_V)
    return out_proj(lay, o, w_o, jnp.zeros((d,), F32), h, mod)


def _dft_cos_sin(n):
    j = np.arange(n, dtype=np.int64)
    m = (j[:, None] * j[None, :]) % n
    ang = 2.0 * np.pi * m.astype(np.float64) / n
    return np.cos(ang), np.sin(ang)


def _fn_stage1_body(gw, h_ref, g_ref, mod_ref, cs_ref, zc_ref, zs_ref):
    u = (_rms(h_ref[...], g_ref[...]) * (1.0 + mod_ref[0, 1:2, :]) + mod_ref[0, 0:1, :]).astype(BF16)
    for gi in range(FN_GROUPS):
        z = jnp.dot(u[:, gi * gw:(gi + 1) * gw], cs_ref[...], preferred_element_type=F32)
        zc_ref[:, gi * gw:(gi + 1) * gw] = z[:, 0:gw].astype(BF16)
        zs_ref[:, gi * gw:(gi + 1) * gw] = z[:, gw:2 * gw].astype(BF16)


def _fn_stage2_body(lay, ct_ref, st_ref, zc_ref, zs_ref, o_ref):
    j = pl.program_id(1)

    def mix(lo, hi):
        f = (jnp.dot(ct_ref[:, lo:hi], zc_ref[lo:hi, :], preferred_element_type=F32)
             - jnp.dot(st_ref[:, lo:hi], zs_ref[lo:hi, :], preferred_element_type=F32))
        o_ref[...] = f.astype(BF16)

    @pl.when(j < lay.cb)
    def _():
        mix(0, lay.ctx)

    @pl.when(j >= lay.cb)
    def _():
        mix(lay.ctx, lay.tt)


def fourier_layer(lay, h, mod, gain, w, b):
    d = lay.d
    gw = d // FN_GROUPS
    cw, sw = _dft_cos_sin(gw)
    csw = jnp.asarray(np.concatenate([cw, sw], axis=1), F32).astype(BF16)
    zc, zs = pl.pallas_call(
        functools.partial(_fn_stage1_body, gw),
        grid=(lay.nblocks,),
        in_specs=[_rows(d), _full((1, d)), _modspec(lay), _full((gw, 2 * gw))],
        out_specs=[_rows(d), _rows(d)],
        out_shape=[jax.ShapeDtypeStruct((lay.n, d), BF16)] * 2,
        compiler_params=_cparams(("parallel",)),
        name="fn_stage1",
    )(h, gain.reshape(1, d), mod, csw)

    ct = np.zeros((lay.tt, lay.tt), np.float64)
    st = np.zeros((lay.tt, lay.tt), np.float64)
    for lo, n in ((0, lay.ctx), (lay.ctx, lay.seq)):
        c, s = _dft_cos_sin(n)
        sc = 1.0 / math.sqrt(n * gw)
        ct[lo:lo + n, lo:lo + n] = c * sc
        st[lo:lo + n, lo:lo + n] = s * sc
    ct = jnp.asarray(ct, F32).astype(BF16)
    st = jnp.asarray(st, F32).astype(BF16)
    zc3 = zc.reshape(lay.b, lay.tt, d)
    zs3 = zs.reshape(lay.b, lay.tt, d)
    f = pl.pallas_call(
        functools.partial(_fn_stage2_body, lay),
        grid=(lay.b, lay.nb),
        in_specs=[pl.BlockSpec((TM, lay.tt), lambda bb, j: (j, 0)),
                  pl.BlockSpec((TM, lay.tt), lambda bb, j: (j, 0)),
                  pl.BlockSpec((None, lay.tt, d), lambda bb, j: (bb, 0, 0)),
                  pl.BlockSpec((None, lay.tt, d), lambda bb, j: (bb, 0, 0))],
        out_specs=pl.BlockSpec((TM, d), lambda bb, j: (bb * lay.nb + j, 0)),
        out_shape=jax.ShapeDtypeStruct((lay.n, d), BF16),
        compiler_params=_cparams(("parallel", "arbitrary")),
        name="fn_stage2",
    )(ct, st, zc3, zs3)
    return out_proj(lay, f, w, b, h, mod)


def _gqa_qkv_body(scale, h_ref, g_ref, mod_ref, w_ref, qn_ref, kn_ref, c_ref, slo_ref, shi_ref,
                  q_ref, k_ref, v_ref):
    u = _rms(h_ref[...], g_ref[...]) * (1.0 + mod_ref[0, 1:2, :]) + mod_ref[0, 0:1, :]
    qkv = _dot(u, w_ref[...])
    c, slo, shi = c_ref[...], slo_ref[...], shi_ref[...]
    n_q = GQA_HEADS * GQA_HEAD
    n_kv = GQA_KV_HEADS * GQA_HEAD
    for hd in range(GQA_HEADS):
        x = _rms(qkv[:, hd * LANES:(hd + 1) * LANES], qn_ref[...])
        q_ref[:, hd * LANES:(hd + 1) * LANES] = (_rope(x, c, slo, shi, GQA_HEAD // 2) * scale).astype(BF16)
    for hd in range(GQA_KV_HEADS):
        x = _rms(qkv[:, n_q + hd * LANES:n_q + (hd + 1) * LANES], kn_ref[...])
        k_ref[:, hd * LANES:(hd + 1) * LANES] = _rope(x, c, slo, shi, GQA_HEAD // 2).astype(BF16)
    v_ref[...] = qkv[:, n_q + n_kv:].astype(BF16)


def gqa_layer(lay, h, mod, gain, w_qkv, q_norm, k_norm, w_o):
    d = lay.d
    tabs = rope_tables(lay, GQA_HEAD)
    n_q = GQA_HEADS * GQA_HEAD
    n_kv = GQA_KV_HEADS * GQA_HEAD
    q, k, v = pl.pallas_call(
        functools.partial(_gqa_qkv_body, float(GQA_HEAD) ** -0.5),
        grid=(lay.nblocks,),
        in_specs=[_rows(d), _full((1, d)), _modspec(lay), _full((d, n_q + 2 * n_kv)), _full((1, LANES)),
                  _full((1, LANES))] + [_posrows(lay)] * 3,
        out_specs=[_rows(n_q), _rows(n_kv), _rows(n_kv)],
        out_shape=[jax.ShapeDtypeStruct((lay.n, n_q), BF16), jax.ShapeDtypeStruct((lay.n, n_kv), BF16),
                   jax.ShapeDtypeStruct((lay.n, n_kv), BF16)],
        compiler_params=_cparams(("parallel",)),
        name="gqa_qkv",
    )(h, gain.reshape(1, d), mod, w_qkv.astype(BF16), q_norm.reshape(1, LANES), k_norm.reshape(1, LANES), *tabs)
    o = attention(lay, q, k, v, GQA_HEADS, GQA_HEADS // GQA_KV_HEADS, GQA_HEAD, GQA_HEAD)
    return out_proj(lay, o, w_o, jnp.zeros((d,), F32), h, mod)


def _ffn_prep_body(h_ref, g_ref, mod_ref, wr_ref, br_ref, v_ref, idx_ref, wt_ref):
    v = (_rms(h_ref[...], g_ref[...]) * (1.0 + mod_ref[0, 4:5, :]) + mod_ref[0, 3:4, :]).astype(BF16)
    v_ref[...] = v
    logit = jnp.dot(v, wr_ref[...], preferred_element_type=F32) + br_ref[...]
    lane = lax.broadcasted_iota(jnp.int32, logit.shape, 1).astype(F32)
    idx = jnp.zeros_like(logit)
    ex = jnp.zeros_like(logit)
    den = 0.0
    m0 = None
    for kx in range(TOP_K):
        m = jnp.max(logit, axis=-1, keepdims=True)
        am = jnp.min(jnp.where(logit == m, lane, float(LANES)), axis=-1, keepdims=True)
        if kx == 0:
            m0 = m
        e = jnp.exp(m - m0)
        den = den + e
        idx = jnp.where(lane == kx, am, idx)
        ex = jnp.where(lane == kx, e, ex)
        logit = jnp.where(lane == am, -jnp.inf, logit)
    idx_ref[...] = idx.astype(jnp.int32)
    wt_ref[...] = ex / den


def ffn_prep(lay, h, gain, mod, w_r, b_r):
    d = lay.d
    ne = w_r.shape[1]
    wr = jnp.pad(w_r, ((0, 0), (0, LANES - ne))).astype(BF16)
    br = jnp.concatenate([b_r, jnp.full((LANES - ne,), -1e30, F32)]).reshape(1, LANES)
    return pl.pallas_call(
        _ffn_prep_body,
        grid=(lay.nblocks,),
        in_specs=[_rows(d), _full((1, d)), _modspec(lay), _full((d, LANES)), _full((1, LANES))],
        out_specs=[_rows(d), _rows(LANES), _rows(LANES)],
        out_shape=[jax.ShapeDtypeStruct((lay.n, d), BF16), jax.ShapeDtypeStruct((lay.n, LANES), jnp.int32),
                   jax.ShapeDtypeStruct((lay.n, LANES), F32)],
        compiler_params=_cparams(("parallel",)),
        name="ffn_prep",
    )(h, gain.reshape(1, d), mod, wr, br)


def _experts_body(be_ref, bv_ref, x_ref, wg_ref, bg_ref, wu_ref, bu_ref, wd_ref, bd_ref, rw_ref, y_ref):
    i = pl.program_id(0)

    @pl.when(bv_ref[i] > 0)
    def _():
        x = x_ref[...]
        gt = jnp.minimum(jnp.dot(x, wg_ref[0], preferred_element_type=F32) + bg_ref[0], SWIGLU_LIMIT)
        up = jnp.clip(jnp.dot(x, wu_ref[0], preferred_element_type=F32) + bu_ref[0], -SWIGLU_LIMIT, SWIGLU_LIMIT)
        act = gt * _sigmoid(SWIGLU_ALPHA * gt) * (up + 1.0)
        y = _dot(act, wd_ref[0]) + bd_ref[0]
        y_ref[...] = y * rw_ref[...]

    @pl.when(bv_ref[i] == 0)
    def _():
        y_ref[...] = jnp.zeros_like(y_ref)


def experts(xs, row_w, blk_e, blk_valid, wg, bg, wu, bu, wd, bd):
    n_rows, d = xs.shape
    ne, _, ff = wg.shape
    r = EXPERT_ROWS
    grid_spec = pltpu.PrefetchScalarGridSpec(
        num_scalar_prefetch=2,
        grid=(n_rows // r,),
        in_specs=[pl.BlockSpec((r, d), lambda i, be, bv: (i, 0)),
                  pl.BlockSpec((1, d, ff), lambda i, be, bv: (be[i], 0, 0)),
                  pl.BlockSpec((1, 1, ff), lambda i, be, bv: (be[i], 0, 0)),
                  pl.BlockSpec((1, d, ff), lambda i, be, bv: (be[i], 0, 0)),
                  pl.BlockSpec((1, 1, ff), lambda i, be, bv: (be[i], 0, 0)),
                  pl.BlockSpec((1, ff, d), lambda i, be, bv: (be[i], 0, 0)),
                  pl.BlockSpec((1, 1, d), lambda i, be, bv: (be[i], 0, 0)),
                  pl.BlockSpec((r, 1), lambda i, be, bv: (i, 0))],
        out_specs=pl.BlockSpec((r, d), lambda i, be, bv: (i, 0)),
    )
    return pl.pallas_call(
        _experts_body,
        grid_spec=grid_spec,
        out_shape=jax.ShapeDtypeStruct((n_rows, d), F32),
        compiler_params=_cparams(("arbitrary",)),
        name="experts",
    )(blk_e, blk_valid, xs, wg, bg.reshape(ne, 1, ff), wu, bu.reshape(ne, 1, ff), wd, bd.reshape(ne, 1, d),
      row_w.reshape(n_rows, 1))


def _ffn_out_body(h_ref, f_ref, mod_ref, out_ref):
    d = h_ref.shape[1]
    f = f_ref[:, 0:d]
    for kx in range(1, TOP_K):
        f = f + f_ref[:, kx * d:(kx + 1) * d]
    out_ref[...] = h_ref[...] + mod_ref[0, 5:6, :] * f


def ffn_out(lay, h, f4, mod):
    d = lay.d
    return pl.pallas_call(
        _ffn_out_body,
        grid=(lay.nblocks,),
        in_specs=[_rows(d), _rows(TOP_K * d), _modspec(lay)],
        out_specs=_rows(d),
        out_shape=jax.ShapeDtypeStruct((lay.n, d), F32),
        compiler_params=_cparams(("parallel",)),
        name="ffn_out",
    )(h, f4, mod)


def moe_layer(lay, h, mod, gain, w_r, b_r, wg, bg, wu, bu, wd, bd):
    ne = w_r.shape[1]
    r = EXPERT_ROWS
    v, idx, wts = ffn_prep(lay, h, gain, mod, w_r, b_r)
    n_slot = lay.n * TOP_K
    slot_e = idx[:, :TOP_K].reshape(n_slot)
    slot_w = wts[:, :TOP_K].reshape(n_slot)
    slot_tok = jnp.repeat(jnp.arange(lay.n, dtype=jnp.int32), TOP_K)
    onehot = (slot_e[:, None] == jnp.arange(ne, dtype=jnp.int32)[None, :]).astype(jnp.int32)
    cum = jnp.cumsum(onehot, axis=0)
    rank = jnp.sum(cum * onehot, axis=1) - 1
    counts = cum[-1]
    padded = (counts + r - 1) // r * r
    pad_end = jnp.cumsum(padded)
    pad_start = pad_end - padded
    dest = pad_start[slot_e] + rank
    n_rows = n_slot + ne * r
    row_tok = jnp.zeros((n_rows,), jnp.int32).at[dest].set(slot_tok)
    row_w = jnp.zeros((n_rows,), F32).at[dest].set(slot_w)
    blk_start = jnp.arange(n_rows // r, dtype=jnp.int32) * r
    blk_e = jnp.minimum(jnp.searchsorted(pad_end, blk_start, side='right'), ne - 1).astype(jnp.int32)
    blk_valid = (blk_start < pad_end[-1]).astype(jnp.int32)
    xs = jnp.take(v, row_tok, axis=0)
    y = experts(xs, row_w, blk_e, blk_valid, wg.astype(BF16), bg, wu.astype(BF16), bu, wd.astype(BF16), bd)
    f4 = jnp.take(y, dest, axis=0).reshape(lay.n, TOP_K * lay.d)
    return ffn_out(lay, h, f4, mod)


def kernel(x, c, ctx, c_ctx, mod_w, mod_b, norm_mix, norm_ffn, router_w, router_b, exp_w_gate, exp_b_gate, exp_w_up, exp_b_up, exp_w_down, exp_b_down, rw_mu, rw_wr, rw_wk, rw_wv, rw_wo, rw_w0, rw_w1, rw_w2, rw_a0, rw_a1, rw_a2, rw_g1, rw_g2, rw_kk, rw_ka, rw_rk, rw_lnx_g, rw_lnx_b, mla_w_down, mla_qa_norm, mla_w_qb, mla_kv_norm, mla_w_kvb, mla_qn_nope, mla_qn_rope, mla_kn_nope, mla_kn_rope, mla_wo, fn_w, fn_b, gqa_w_qkv, gqa_q_norm, gqa_k_norm, gqa_wo):
    bsz, seq, d = x.shape
    ctx_len = ctx.shape[1]
    depth = mod_w.shape[0]
    lay = Layout(bsz, ctx_len, seq, d)
    assert bsz < 16 and d // PAIR * PAIR == d

    cin = jnp.zeros((16, d), F32).at[:bsz].set(c).at[bsz].set(c_ctx)
    mod_all = modulation(cin, mod_w, mod_b).reshape(depth, 16, 6, d)

    h = jnp.concatenate([ctx, x], axis=1).reshape(lay.n, d)
    n_mixers = 4
    for i in range(depth):
        m, j = i % n_mixers, i // n_mixers
        mod = mod_all[i]
        if m == 0:
            h = rwkv_layer(lay, h, mod, norm_mix[i], rw_mu[j], rw_wr[j], rw_wk[j], rw_wv[j], rw_wo[j], rw_w0[j],
                           rw_w1[j], rw_w2[j], rw_a0[j], rw_a1[j], rw_a2[j], rw_g1[j], rw_g2[j], rw_kk[j],
                           rw_ka[j], rw_rk[j], rw_lnx_g[j], rw_lnx_b[j])
        elif m == 1:
            h = mla_layer(lay, h, mod, norm_mix[i], mla_w_down[j], mla_qa_norm[j], mla_w_qb[j], mla_kv_norm[j],
                          mla_w_kvb[j], mla_qn_nope[j], mla_qn_rope[j], mla_kn_nope[j], mla_kn_rope[j], mla_wo[j])
        elif m == 2:
            h = fourier_layer(lay, h, mod, norm_mix[i], fn_w[j], fn_b[j])
        else:
            h = gqa_layer(lay, h, mod, norm_mix[i], gqa_w_qkv[j], gqa_q_norm[j], gqa_k_norm[j], gqa_wo[j])
        h = moe_layer(lay, h, mod, norm_ffn[i], router_w[i], router_b[i], exp_w_gate[i], exp_b_gate[i],
                      exp_w_up[i], exp_b_up[i], exp_w_down[i], exp_b_down[i])
    return h.reshape(bsz, lay.tt, d)[:, ctx_len:, :]
```

```python
import functools
import math

import numpy as np
import jax
import jax.numpy as jnp
from jax import lax
from jax.experimental import pallas as pl
from jax.experimental.pallas import tpu as pltpu

F32 = jnp.float32
BF16 = jnp.bfloat16

LANES = 128
SUBLANES = 8
VMEM_LIMIT_BYTES = 56 * 1024 * 1024

GRID_W = 64
ROPE_THETA = 10000.0
NORM_EPS = 1e-6
RW_HEAD = 64
RW_GN_EPS = 64e-5
MLA_HEADS = 16
MLA_Q_RANK = 512
MLA_KV_RANK = 512
MLA_NOPE = 128
MLA_ROPE = 64
MLA_V = 128
FN_GROUPS = 8
GQA_HEADS = 16
GQA_KV_HEADS = 4
GQA_HEAD = 128
TOP_K = 4
SWIGLU_LIMIT = 7.0
SWIGLU_ALPHA = 1.702

TM = 256
WKV_CHUNK = 64
WKV_GROUP = 16
EXPERT_ROWS = 256
PAIR = 2 * RW_HEAD


def _cparams(sem):
    return pltpu.CompilerParams(dimension_semantics=sem, vmem_limit_bytes=VMEM_LIMIT_BYTES)


def _dot(a, b):
    return jnp.dot(a.astype(BF16), b.astype(BF16), preferred_element_type=F32)


def _dot_nt(a, b):
    return lax.dot_general(a.astype(BF16), b.astype(BF16), (((1,), (1,)), ((), ())),
                           preferred_element_type=F32)


def _rms(x, gain):
    return x * lax.rsqrt(jnp.mean(x * x, axis=-1, keepdims=True) + NORM_EPS) * gain


def _sigmoid(x):
    return 1.0 / (1.0 + jnp.exp(-x))


class Layout:
    def __init__(self, batch, ctx_len, seq, d):
        self.b, self.ctx, self.seq, self.d = batch, ctx_len, seq, d
        self.tt = ctx_len + seq
        self.n = batch * self.tt
        assert ctx_len % TM == 0 and seq % TM == 0
        self.nb = self.tt // TM
        self.cb = ctx_len // TM
        self.nblocks = self.n // TM

    def seg(self, i):
        return jnp.where(i % self.nb < self.cb, self.b, i // self.nb)


def _rows(ncols, tm=TM):
    return pl.BlockSpec((tm, ncols), lambda i: (i, 0))


def _full(shape):
    nd = len(shape)
    return pl.BlockSpec(shape, lambda i: (0,) * nd)


def _modspec(lay):
    return pl.BlockSpec((1, 6, lay.d), lambda i: (lay.seg(i), 0, 0))


def _posrows(lay, ncols=LANES):
    return pl.BlockSpec((TM, ncols), lambda i: (i % lay.nb, 0))


def _pairs(tm=TM):
    return pl.BlockSpec((None, tm, PAIR), lambda i: (0, i, 0))


def _mod_body(c_ref, w_ref, b_ref, o_ref):
    c = c_ref[...]
    s = c * _sigmoid(c)
    o_ref[0] = _dot(s, w_ref[0]) + b_ref[0]


def modulation(cin, mod_w, mod_b):
    depth, d, n6 = mod_w.shape
    tn = 1024
    return pl.pallas_call(
        _mod_body,
        grid=(depth, n6 // tn),
        in_specs=[pl.BlockSpec((16, d), lambda l, j: (0, 0)),
                  pl.BlockSpec((1, d, tn), lambda l, j: (l, 0, j)),
                  pl.BlockSpec((1, 1, tn), lambda l, j: (l, 0, j))],
        out_specs=pl.BlockSpec((1, 16, tn), lambda l, j: (l, 0, j)),
        out_shape=jax.ShapeDtypeStruct((depth, 16, n6), F32),
        compiler_params=_cparams(("parallel", "parallel")),
        name="modulation",
    )(cin, mod_w, mod_b.reshape(depth, 1, n6))


def _rw_prep_body(lay, h_ref, hp_ref, hn_ref, g_ref, mod_ref, mu_ref, *outs):
    j = pl.program_id(0) % lay.nb
    seg_start = jnp.logical_or(j == 0, j == lay.cb)
    seg_end = jnp.logical_or(j == lay.cb - 1, j == lay.nb - 1)
    gain = g_ref[...]
    shift, scale = mod_ref[0, 0:1, :], mod_ref[0, 1:2, :]

    def norm_mod(x):
        return _rms(x, gain) * (1.0 + scale) + shift

    u = norm_mod(h_ref[...])
    prev = jnp.where(seg_start, 0.0, norm_mod(hp_ref[SUBLANES - 1:SUBLANES, :]))
    nxt = jnp.where(seg_end, 0.0, norm_mod(hn_ref[0:1, :]))
    row = lax.broadcasted_iota(jnp.int32, u.shape, 0)
    up = jnp.where(row == 0, prev, pltpu.roll(u, 1, 0))
    un = jnp.where(row == TM - 1, nxt, pltpu.roll(u, TM - 1, 0))
    du = 0.5 * (up + un) - u
    for n, o_ref in enumerate(outs):
        o_ref[...] = (u + du * mu_ref[n:n + 1, :]).astype(BF16)


def rw_prep(lay, h, gain, mod, mu):
    d = lay.d
    r8 = TM // SUBLANES
    last8 = lay.n // SUBLANES - 1
    mu8 = jnp.concatenate([mu, jnp.zeros((2, d), F32)], axis=0)
    return pl.pallas_call(
        functools.partial(_rw_prep_body, lay),
        grid=(lay.nblocks,),
        in_specs=[_rows(d),
                  pl.BlockSpec((SUBLANES, d), lambda i: (jnp.maximum(i * r8 - 1, 0), 0)),
                  pl.BlockSpec((SUBLANES, d), lambda i: (jnp.minimum((i + 1) * r8, last8), 0)),
                  _full((1, d)), _modspec(lay), _full((8, d))],
        out_specs=[_rows(d)] * 6,
        out_shape=[jax.ShapeDtypeStruct((lay.n, d), BF16)] * 6,
        compiler_params=_cparams(("parallel",)),
        name="rw_prep",
    )(h, h, h, gain.reshape(1, d), mod, mu8)


def _linear_pairs_body(npairs, x_ref, w_ref, o_ref):
    y = _dot(x_ref[...], w_ref[...])
    for p in range(npairs):
        o_ref[p] = y[:, p * PAIR:(p + 1) * PAIR]


def linear_pairs(lay, x, w):
    d = lay.d
    npairs = d // PAIR
    return pl.pallas_call(
        functools.partial(_linear_pairs_body, npairs),
        grid=(lay.nblocks,),
        in_specs=[_rows(d), _full((d, d))],
        out_specs=pl.BlockSpec((npairs, TM, PAIR), lambda i: (0, i, 0)),
        out_shape=jax.ShapeDtypeStruct((npairs, lay.n, PAIR), F32),
        compiler_params=_cparams(("parallel",)),
        name="rw_linear",
    )(x, w)


def _rw_lora_body(npairs, xw_ref, xa_ref, xg_ref, w1_ref, w2_ref, a1_ref, a2_ref, g1_ref, g2_ref,
                  w0_ref, a0_ref, lw_ref, ag_ref, g_ref):
    hw = jnp.tanh(_dot(xw_ref[...], w1_ref[...]))
    ha = _dot(xa_ref[...], a1_ref[...])
    hg = _sigmoid(_dot(xg_ref[...], g1_ref[...]))
    rw = w1_ref.shape[1] // 2
    rg = g1_ref.shape[1] // 2
    for dr in range(2):
        z = w0_ref[dr:dr + 1, :] + _dot(hw[:, dr * rw:(dr + 1) * rw], w2_ref[dr])
        lw = -math.exp(-0.5) * _sigmoid(z)
        a = _sigmoid(a0_ref[dr:dr + 1, :] + _dot(ha[:, dr * rw:(dr + 1) * rw], a2_ref[dr]))
        g = _dot(hg[:, dr * rg:(dr + 1) * rg], g2_ref[dr])
        for p in range(npairs):
            sl = slice(p * PAIR, (p + 1) * PAIR)
            lw_ref[dr, p] = lw[:, sl]
            ag_ref[dr, p] = a[:, sl]
            g_ref[dr, p] = g[:, sl]


def _pad_lora(w1, w2):
    r = w1.shape[2]
    rp = -(-r // LANES) * LANES
    w1p = jnp.pad(w1, ((0, 0), (0, 0), (0, rp - r)))
    w1p = jnp.concatenate([w1p[0], w1p[1]], axis=1).astype(BF16)
    w2p = jnp.pad(w2, ((0, 0), (0, rp - r), (0, 0))).astype(BF16)
    return w1p, w2p


def rw_lora(lay, xw, xa, xg, w0, w1, w2, a0, a1, a2, g1, g2):
    d = lay.d
    npairs = d // PAIR
    w1p, w2p = _pad_lora(w1, w2)
    a1p, a2p = _pad_lora(a1, a2)
    g1p, g2p = _pad_lora(g1, g2)
    ospec = pl.BlockSpec((2, npairs, TM, PAIR), lambda i: (0, 0, i, 0))
    oshape = jax.ShapeDtypeStruct((2, npairs, lay.n, PAIR), F32)
    return pl.pallas_call(
        functools.partial(_rw_lora_body, npairs),
        grid=(lay.nblocks,),
        in_specs=[_rows(d)] * 3 + [_full(w1p.shape), _full(w2p.shape), _full(a1p.shape), _full(a2p.shape),
                                    _full(g1p.shape), _full(g2p.shape), _full((2, d)), _full((2, d))],
        out_specs=[ospec] * 3,
        out_shape=[oshape] * 3,
        compiler_params=_cparams(("parallel",)),
        name="rw_lora",
    )(xw, xa, xg, w1p, w2p, a1p, a2p, g1p, g2p, w0, a0)


def _bd(z, head0):
    return jnp.concatenate([jnp.where(head0, z, 0.0), jnp.where(head0, 0.0, z)], axis=0)


def _head_sum(x, head0):
    s0 = jnp.sum(jnp.where(head0, x, 0.0), axis=-1, keepdims=True)
    s1 = jnp.sum(jnp.where(head0, 0.0, x), axis=-1, keepdims=True)
    return jnp.where(head0, s0, s1)


def _wkv_group_chunk(rev, tiles, par, hts):
    c = WKV_CHUNK
    rng = range(len(tiles))
    t = lax.broadcasted_iota(jnp.int32, (c, PAIR), 0)
    lane = lax.broadcasted_iota(jnp.int32, (c, PAIR), 1)
    s = lane % RW_HEAD
    head0 = lane < RW_HEAD
    strict = (s > t) if rev else (s < t)
    incl = (s >= t) if rev else (s <= t)
    eye = jnp.where(s == t, 1.0, 0.0)
    rr = lax.broadcasted_iota(jnp.int32, (PAIR, PAIR), 0) // RW_HEAD
    cc = lax.broadcasted_iota(jnp.int32, (PAIR, PAIR), 1) // RW_HEAD
    same_head = rr == cc

    def prep(q):
        r, k, v, lw, ag, _ = tiles[q]
        kk_p, ka_p = par[q][0], par[q][1]
        kkf = k * kk_p
        kk = kkf / jnp.maximum(jnp.sqrt(_head_sum(kkf * kkf, head0)), 1e-12)
        kd = k * (1.0 + (ag - 1.0) * ka_p)
        bvec = kk * ag
        cs = lw
        for sh in (1, 2, 4, 8, 16, 32):
            if rev:
                cs = cs + jnp.where(t + sh < c, pltpu.roll(cs, c - sh, 0), 0.0)
            else:
                cs = cs + jnp.where(t >= sh, pltpu.roll(cs, sh, 0), 0.0)
        cl = cs[0:1, :] if rev else cs[c - 1:c, :]
        e_neg = jnp.exp(-cs)
        e_end = jnp.exp(cl - cs)
        ar = jnp.concatenate([-kk * jnp.exp(cs - lw), r * jnp.exp(cs)], axis=0).astype(BF16)
        bk = jnp.concatenate([_bd(bvec * e_neg, head0), _bd(kd * e_neg, head0)], axis=0).astype(BF16)
        bk_end = jnp.concatenate([bvec * e_end, kd * e_end], axis=0).astype(BF16)
        return dict(ar=ar, bk=bk, bk_end=bk_end, kd=kd, decay=jnp.exp(cl), vbd=_bd(v, head0).astype(BF16))

    st = [prep(q) for q in rng]
    p = [_dot_nt(st[q]["ar"], st[q]["bk"]) for q in rng]
    arh = [_dot_nt(st[q]["ar"], hts[q]) for q in rng]
    l_ab = [jnp.where(strict, p[q][0:c, 0:PAIR], 0.0) for q in rng]
    l_ak = [jnp.where(strict, p[q][0:c, PAIR:2 * PAIR], 0.0) for q in rng]
    m_r = [jnp.concatenate([jnp.where(incl, p[q][c:2 * c, 0:PAIR], 0.0),
                            jnp.where(incl, p[q][c:2 * c, PAIR:2 * PAIR], 0.0)], axis=1).astype(BF16) for q in rng]
    x = [arh[q][0:c] + _dot(l_ak[q], st[q]["vbd"]) for q in rng]
    tinv = [eye + l_ab[q] for q in rng]
    pw = l_ab
    for _ in range(5):
        pw = [_dot(pw[q], _bd(pw[q], head0)) for q in rng]
        tinv = [tinv[q] + _dot(tinv[q], _bd(pw[q], head0)) for q in rng]
    u = [_dot(tinv[q], _bd(x[q], head0)) for q in rng]
    y = [arh[q][c:2 * c] + _dot(m_r[q], jnp.concatenate([_bd(u[q], head0).astype(BF16), st[q]["vbd"]], axis=0))
         for q in rng]
    upd = [_dot(jnp.concatenate([u[q], tiles[q][2]], axis=0).T, st[q]["bk_end"]) for q in rng]
    ht_new = [hts[q] * st[q]["decay"] + jnp.where(same_head, upd[q], 0.0) for q in rng]

    outs = []
    inv_n = 1.0 / RW_HEAD
    for q in rng:
        r, _, v, _, _, g = tiles[q]
        rk_p, lg_p, lb_p = par[q][2], par[q][3], par[q][4]
        mu = _head_sum(y[q], head0) * inv_n
        yc = y[q] - mu
        var = _head_sum(yc * yc, head0) * inv_n
        bonus = _head_sum(r * st[q]["kd"] * rk_p, head0) * v
        outs.append((yc * lax.rsqrt(var + RW_GN_EPS) * lg_p + lb_p + bonus) * g)
    return outs, ht_new


def _wkv_body(rev, npairs, r_ref, k_ref, v_ref, lw_ref, ag_ref, g_ref, kk_ref, ka_ref, rk_ref, lg_ref, lb_ref,
              o_ref, ht_ref):
    @pl.when(pl.program_id(1) == 0)
    def _():
        ht_ref[...] = jnp.zeros_like(ht_ref)

    def group(gi, carry):
        ps = [gi * WKV_GROUP + q for q in range(WKV_GROUP)]
        tiles = [(r_ref[p], k_ref[p], v_ref[p], lw_ref[0, p], ag_ref[0, p], g_ref[0, p]) for p in ps]
        par = [(kk_ref[p], ka_ref[p], rk_ref[p], lg_ref[p], lb_ref[p]) for p in ps]
        outs, hts = _wkv_group_chunk(rev, tiles, par, [ht_ref[p] for p in ps])
        for q, p in enumerate(ps):
            o_ref[p] = outs[q]
            ht_ref[p] = hts[q]
        return carry

    lax.fori_loop(0, npairs // WKV_GROUP, group, 0)


def wkv(lay, rev, r, k, v, lw, ag, g, params):
    npairs = lay.d // PAIR
    c = WKV_CHUNK
    nch = lay.tt // c
    cch = lay.ctx // c
    dr = 1 if rev else 0

    def rowblk(b, ci):
        if rev:
            return b * nch + jnp.where(ci < cch, cch - 1 - ci, nch + cch - 1 - ci)
        return b * nch + ci

    tok = pl.BlockSpec((npairs, c, PAIR), lambda b, ci: (0, rowblk(b, ci), 0))
    tokd = pl.BlockSpec((1, npairs, c, PAIR), lambda b, ci: (dr, 0, rowblk(b, ci), 0))
    par = pl.BlockSpec((npairs, 1, PAIR), lambda b, ci: (0, 0, 0))
    return pl.pallas_call(
        functools.partial(_wkv_body, rev, npairs),
        grid=(lay.b, nch),
        in_specs=[tok, tok, tok, tokd, tokd, tokd] + [par] * 5,
        out_specs=tok,
        out_shape=jax.ShapeDtypeStruct((npairs, lay.n, PAIR), F32),
        scratch_shapes=[pltpu.VMEM((npairs, PAIR, PAIR), F32)],
        compiler_params=_cparams(("parallel", "arbitrary")),
        name="wkv_rev" if rev else "wkv_fwd",
    )(r, k, v, lw, ag, g, *params)


def _rw_out_body(npairs, o0_ref, o1_ref, h_ref, mod_ref, w_ref, out_ref):
    acc = jnp.concatenate([o0_ref[p] + o1_ref[p] for p in range(npairs)], axis=-1)
    out_ref[...] = h_ref[...] + mod_ref[0, 2:3, :] * _dot(acc, w_ref[...])


def rw_out(lay, o0, o1, h, mod, w):
    d = lay.d
    npairs = d // PAIR
    pm = pl.BlockSpec((npairs, TM, PAIR), lambda i: (0, i, 0))
    return pl.pallas_call(
        functools.partial(_rw_out_body, npairs),
        grid=(lay.nblocks,),
        in_specs=[pm, pm, _rows(d), _modspec(lay), _full((d, d))],
        out_specs=_rows(d),
        out_shape=jax.ShapeDtypeStruct((lay.n, d), F32),
        compiler_params=_cparams(("parallel",)),
        name="rw_out",
    )(o0, o1, h, mod, w)


def rwkv_layer(lay, h, mod, gain, mu, w_r, w_k, w_v, w_o, w0, w1, w2, a0, a1, a2, g1, g2, k_k, k_a, r_k,
               lnx_g, lnx_b):
    npairs = lay.d // PAIR
    xr, xw, xk, xv, xa, xg = rw_prep(lay, h, gain, mod, mu)
    r = linear_pairs(lay, xr, w_r.astype(BF16))
    k = linear_pairs(lay, xk, w_k.astype(BF16))
    v = linear_pairs(lay, xv, w_v.astype(BF16))
    lw, ag, g = rw_lora(lay, xw, xa, xg, w0, w1, w2, a0, a1, a2, g1, g2)
    params = [t.reshape(npairs, 1, PAIR) for t in (k_k, k_a, r_k.reshape(-1), lnx_g, lnx_b)]
    o0 = wkv(lay, False, r, k, v, lw, ag, g, params)
    o1 = wkv(lay, True, r, k, v, lw, ag, g, params)
    return rw_out(lay, o0, o1, h, mod, w_o.astype(BF16))


def rope_tables(lay, rot_dim):
    quarter = rot_dim // 4
    half = rot_dim // 2
    t = np.arange(lay.seq)
    inv = ROPE_THETA ** (-np.arange(quarter, dtype=np.float32) / quarter)
    ang = np.concatenate([(t // GRID_W)[:, None].astype(np.float32) * inv,
                          (t % GRID_W)[:, None].astype(np.float32) * inv], axis=-1)
    ang = jnp.asarray(ang, F32)
    cos, sin = jnp.cos(ang), jnp.sin(ang)
    pad = LANES - rot_dim
    zer = jnp.zeros((lay.seq, half), F32)
    c = jnp.concatenate([cos, cos, jnp.ones((lay.seq, pad), F32)], axis=-1)
    s_lo = jnp.concatenate([-sin, zer, jnp.zeros((lay.seq, pad), F32)], axis=-1)
    s_hi = jnp.concatenate([zer, sin, jnp.zeros((lay.seq, pad), F32)], axis=-1)

    def with_ctx(tab, fill):
        return jnp.concatenate([jnp.full((lay.ctx, LANES), fill, F32), tab], axis=0)

    return with_ctx(c, 1.0), with_ctx(s_lo, 0.0), with_ctx(s_hi, 0.0)


def _rope(x, c, s_lo, s_hi, half):
    return x * c + pltpu.roll(x, LANES - half, 1) * s_lo + pltpu.roll(x, half, 1) * s_hi


def _attn_body(ctx, tq, q_ref, k_ref, v_ref, o_ref):
    s = _dot_nt(q_ref[...], k_ref[...])
    qrow = pl.program_id(2) * tq + lax.broadcasted_iota(jnp.int32, s.shape, 0)
    kcol = lax.broadcasted_iota(jnp.int32, s.shape, 1)
    s = jnp.where(jnp.logical_or(qrow >= ctx, kcol < ctx), s, -1e30)
    m = jnp.max(s, axis=-1, keepdims=True)
    p = jnp.exp(s - m)
    l = jnp.sum(p, axis=-1, keepdims=True)
    o_ref[...] = (_dot(p, v_ref[...]) / l).astype(o_ref.dtype)


def attention(lay, q, k, v, n_heads, kv_group, dq, dv, tq=768):
    nq = lay.tt // tq
    k3 = k.reshape(lay.b, lay.tt, k.shape[1])
    v3 = v.reshape(lay.b, lay.tt, v.shape[1])
    return pl.pallas_call(
        functools.partial(_attn_body, lay.ctx, tq),
        grid=(lay.b, n_heads, nq),
        in_specs=[pl.BlockSpec((tq, dq), lambda b, h, i: (b * nq + i, h)),
                  pl.BlockSpec((None, lay.tt, dq), lambda b, h, i: (b, 0, h // kv_group)),
                  pl.BlockSpec((None, lay.tt, dv), lambda b, h, i: (b, 0, h // kv_group))],
        out_specs=pl.BlockSpec((tq, dv), lambda b, h, i: (b * nq + i, h)),
        out_shape=jax.ShapeDtypeStruct((lay.n, n_heads * dv), BF16),
        compiler_params=_cparams(("parallel", "parallel", "arbitrary")),
        name="attention",
    )(q, k3, v3)


def _out_proj_body(x_ref, w_ref, b_ref, h_ref, mod_ref, out_ref):
    y = _dot(x_ref[...], w_ref[...]) + b_ref[...]
    out_ref[...] = h_ref[...] + mod_ref[0, 2:3, :] * y


def out_proj(lay, x, w, bias, h, mod):
    d = lay.d
    kdim = x.shape[1]
    return pl.pallas_call(
        _out_proj_body,
        grid=(lay.nblocks,),
        in_specs=[_rows(kdim), _full((kdim, d)), _full((1, d)), _rows(d), _modspec(lay)],
        out_specs=_rows(d),
        out_shape=jax.ShapeDtypeStruct((lay.n, d), F32),
        compiler_params=_cparams(("parallel",)),
        name="out_proj",
    )(x, w.astype(BF16), bias.reshape(1, d), h, mod)


def _mla_down_body(h_ref, g_ref, mod_ref, w_ref, qan_ref, kvn_ref, krn_ref, c_ref, slo_ref, shi_ref,
                   q_ref, ckv_ref, kr_ref):
    u = _rms(h_ref[...], g_ref[...]) * (1.0 + mod_ref[0, 1:2, :]) + mod_ref[0, 0:1, :]
    down = _dot(u, w_ref[...])
    q_ref[...] = _rms(down[:, 0:MLA_Q_RANK], qan_ref[...]).astype(BF16)
    ckv_ref[...] = _rms(down[:, MLA_Q_RANK:MLA_Q_RANK + MLA_KV_RANK], kvn_ref[...]).astype(BF16)
    kr = down[:, MLA_Q_RANK + MLA_KV_RANK:]
    ms = jnp.sum(kr * kr, axis=-1, keepdims=True) * (1.0 / MLA_ROPE)
    kr = kr * lax.rsqrt(ms + NORM_EPS) * krn_ref[...]
    kr_ref[...] = _rope(kr, c_ref[...], slo_ref[...], shi_ref[...], MLA_ROPE // 2).astype(BF16)


def _mla_q_body(scale, x_ref, w_ref, nn_ref, rn_ref, c_ref, slo_ref, shi_ref, q_ref):
    q = _dot(x_ref[...], w_ref[...])
    hw = 2 * LANES
    for hd in range(MLA_HEADS):
        nope = q[:, hd * hw:hd * hw + LANES]
        rope = q[:, hd * hw + LANES:(hd + 1) * hw]
        nope = _rms(nope, nn_ref[...])
        ms = jnp.sum(rope * rope, axis=-1, keepdims=True) * (1.0 / MLA_ROPE)
        rope = rope * lax.rsqrt(ms + NORM_EPS) * rn_ref[...]
        rope = _rope(rope, c_ref[...], slo_ref[...], shi_ref[...], MLA_ROPE // 2)
        q_ref[:, hd * hw:hd * hw + LANES] = (nope * scale).astype(BF16)
        q_ref[:, hd * hw + LANES:(hd + 1) * hw] = (rope * scale).astype(BF16)


def _mla_kv_body(x_ref, w_ref, kn_ref, kr_ref, k_ref, v_ref):
    kvb = _dot(x_ref[...], w_ref[...])
    hw = 2 * LANES
    kr = kr_ref[...]
    for hd in range(MLA_HEADS):
        k_ref[:, hd * hw:hd * hw + LANES] = _rms(kvb[:, hd * LANES:(hd + 1) * LANES], kn_ref[...]).astype(BF16)
        k_ref[:, hd * hw + LANES:(hd + 1) * hw] = kr
    v_ref[...] = kvb[:, MLA_HEADS * LANES:].astype(BF16)


def mla_layer(lay, h, mod, gain, w_down, qa_norm, w_qb, kv_norm, w_kvb, qn_nope, qn_rope, kn_nope, kn_rope, w_o):
    d = lay.d
    tabs = rope_tables(lay, MLA_ROPE)
    pad64 = lambda t: jnp.concatenate([t, jnp.zeros((LANES - MLA_ROPE,), F32)]).reshape(1, LANES)
    wd = jnp.pad(w_down, ((0, 0), (0, LANES - MLA_ROPE))).astype(BF16)
    nd = wd.shape[1]
    q_lat, ckv, kr = pl.pallas_call(
        _mla_down_body,
        grid=(lay.nblocks,),
        in_specs=[_rows(d), _full((1, d)), _modspec(lay), _full((d, nd)), _full((1, MLA_Q_RANK)),
                  _full((1, MLA_KV_RANK)), _full((1, LANES))] + [_posrows(lay)] * 3,
        out_specs=[_rows(MLA_Q_RANK), _rows(MLA_KV_RANK), _rows(LANES)],
        out_shape=[jax.ShapeDtypeStruct((lay.n, MLA_Q_RANK), BF16), jax.ShapeDtypeStruct((lay.n, MLA_KV_RANK), BF16),
                   jax.ShapeDtypeStruct((lay.n, LANES), BF16)],
        compiler_params=_cparams(("parallel",)),
        name="mla_down",
    )(h, gain.reshape(1, d), mod, wd, qa_norm.reshape(1, -1), kv_norm.reshape(1, -1), pad64(kn_rope), *tabs)

    hq = MLA_NOPE + MLA_ROPE
    wq = w_qb.reshape(MLA_Q_RANK, MLA_HEADS, hq)
    wq = jnp.pad(wq, ((0, 0), (0, 0), (0, 2 * LANES - hq))).reshape(MLA_Q_RANK, MLA_HEADS * 2 * LANES).astype(BF16)
    dq = 2 * LANES
    scale = float(hq) ** -0.5
    q = pl.pallas_call(
        functools.partial(_mla_q_body, scale),
        grid=(lay.nblocks,),
        in_specs=[_rows(MLA_Q_RANK), _full(wq.shape), _full((1, LANES)), _full((1, LANES))] + [_posrows(lay)] * 3,
        out_specs=_rows(MLA_HEADS * dq),
        out_shape=jax.ShapeDtypeStruct((lay.n, MLA_HEADS * dq), BF16),
        compiler_params=_cparams(("parallel",)),
        name="mla_q",
    )(q_lat, wq, qn_nope.reshape(1, LANES), pad64(qn_rope), *tabs)

    wkv_ = w_kvb.reshape(MLA_KV_RANK, MLA_HEADS, MLA_NOPE + MLA_V)
    wkv_ = jnp.concatenate([wkv_[:, :, :MLA_NOPE].reshape(MLA_KV_RANK, -1),
                            wkv_[:, :, MLA_NOPE:].reshape(MLA_KV_RANK, -1)], axis=1).astype(BF16)
    k, v = pl.pallas_call(
        _mla_kv_body,
        grid=(lay.nblocks,),
        in_specs=[_rows(MLA_KV_RANK), _full(wkv_.shape), _full((1, LANES)), _rows(LANES)],
        out_specs=[_rows(MLA_HEADS * dq), _rows(MLA_HEADS * MLA_V)],
        out_shape=[jax.ShapeDtypeStruct((lay.n, MLA_HEADS * dq), BF16),
                   jax.ShapeDtypeStruct((lay.n, MLA_HEADS * MLA_V), BF16)],
        compiler_params=_cparams(("parallel",)),
        name="mla_kv",
    )(ckv, wkv_, kn_nope.reshape(1, LANES), kr)

    o = attention(lay, q, k, v, MLA_HEADS, 1, dq, MLA_V)
    return out_proj(lay, o, w_o, jnp.zeros((d,), F32), h, mod)


def _dft_cos_sin(n):
    j = np.arange(n, dtype=np.int64)
    m = (j[:, None] * j[None, :]) % n
    ang = 2.0 * np.pi * m.astype(np.float64) / n
    return np.cos(ang), np.sin(ang)


def _fn_stage1_body(gw, h_ref, g_ref, mod_ref, cs_ref, zc_ref, zs_ref):
    u = (_rms(h_ref[...], g_ref[...]) * (1.0 + mod_ref[0, 1:2, :]) + mod_ref[0, 0:1, :]).astype(BF16)
    for gi in range(FN_GROUPS):
        z = jnp.dot(u[:, gi * gw:(gi + 1) * gw], cs_ref[...], preferred_element_type=F32)
        zc_ref[:, gi * gw:(gi + 1) * gw] = z[:, 0:gw].astype(BF16)
        zs_ref[:, gi * gw:(gi + 1) * gw] = z[:, gw:2 * gw].astype(BF16)


def _fn_stage2_body(lay, ct_ref, st_ref, zc_ref, zs_ref, o_ref):
    j = pl.program_id(1)

    def mix(lo, hi):
        f = (jnp.dot(ct_ref[:, lo:hi], zc_ref[lo:hi, :], preferred_element_type=F32)
             - jnp.dot(st_ref[:, lo:hi], zs_ref[lo:hi, :], preferred_element_type=F32))
        o_ref[...] = f.astype(BF16)

    @pl.when(j < lay.cb)
    def _():
        mix(0, lay.ctx)

    @pl.when(j >= lay.cb)
    def _():
        mix(lay.ctx, lay.tt)


def fourier_layer(lay, h, mod, gain, w, b):
    d = lay.d
    gw = d // FN_GROUPS
    cw, sw = _dft_cos_sin(gw)
    csw = jnp.asarray(np.concatenate([cw, sw], axis=1), F32).astype(BF16)
    zc, zs = pl.pallas_call(
        functools.partial(_fn_stage1_body, gw),
        grid=(lay.nblocks,),
        in_specs=[_rows(d), _full((1, d)), _modspec(lay), _full((gw, 2 * gw))],
        out_specs=[_rows(d), _rows(d)],
        out_shape=[jax.ShapeDtypeStruct((lay.n, d), BF16)] * 2,
        compiler_params=_cparams(("parallel",)),
        name="fn_stage1",
    )(h, gain.reshape(1, d), mod, csw)

    ct = np.zeros((lay.tt, lay.tt), np.float64)
    st = np.zeros((lay.tt, lay.tt), np.float64)
    for lo, n in ((0, lay.ctx), (lay.ctx, lay.seq)):
        c, s = _dft_cos_sin(n)
        sc = 1.0 / math.sqrt(n * gw)
        ct[lo:lo + n, lo:lo + n] = c * sc
        st[lo:lo + n, lo:lo + n] = s * sc
    ct = jnp.asarray(ct, F32).astype(BF16)
    st = jnp.asarray(st, F32).astype(BF16)
    zc3 = zc.reshape(lay.b, lay.tt, d)
    zs3 = zs.reshape(lay.b, lay.tt, d)
    f = pl.pallas_call(
        functools.partial(_fn_stage2_body, lay),
        grid=(lay.b, lay.nb),
        in_specs=[pl.BlockSpec((TM, lay.tt), lambda bb, j: (j, 0)),
                  pl.BlockSpec((TM, lay.tt), lambda bb, j: (j, 0)),
                  pl.BlockSpec((None, lay.tt, d), lambda bb, j: (bb, 0, 0)),
                  pl.BlockSpec((None, lay.tt, d), lambda bb, j: (bb, 0, 0))],
        out_specs=pl.BlockSpec((TM, d), lambda bb, j: (bb * lay.nb + j, 0)),
        out_shape=jax.ShapeDtypeStruct((lay.n, d), BF16),
        compiler_params=_cparams(("parallel", "arbitrary")),
        name="fn_stage2",
    )(ct, st, zc3, zs3)
    return out_proj(lay, f, w, b, h, mod)


def _gqa_qkv_body(scale, h_ref, g_ref, mod_ref, w_ref, qn_ref, kn_ref, c_ref, slo_ref, shi_ref,
                  q_ref, k_ref, v_ref):
    u = _rms(h_ref[...], g_ref[...]) * (1.0 + mod_ref[0, 1:2, :]) + mod_ref[0, 0:1, :]
    qkv = _dot(u, w_ref[...])
    c, slo, shi = c_ref[...], slo_ref[...], shi_ref[...]
    n_q = GQA_HEADS * GQA_HEAD
    n_kv = GQA_KV_HEADS * GQA_HEAD
    for hd in range(GQA_HEADS):
        x = _rms(qkv[:, hd * LANES:(hd + 1) * LANES], qn_ref[...])
        q_ref[:, hd * LANES:(hd + 1) * LANES] = (_rope(x, c, slo, shi, GQA_HEAD // 2) * scale).astype(BF16)
    for hd in range(GQA_KV_HEADS):
        x = _rms(qkv[:, n_q + hd * LANES:n_q + (hd + 1) * LANES], kn_ref[...])
        k_ref[:, hd * LANES:(hd + 1) * LANES] = _rope(x, c, slo, shi, GQA_HEAD // 2).astype(BF16)
    v_ref[...] = qkv[:, n_q + n_kv:].astype(BF16)


def gqa_layer(lay, h, mod, gain, w_qkv, q_norm, k_norm, w_o):
    d = lay.d
    tabs = rope_tables(lay, GQA_HEAD)
    n_q = GQA_HEADS * GQA_HEAD
    n_kv = GQA_KV_HEADS * GQA_HEAD
    q, k, v = pl.pallas_call(
        functools.partial(_gqa_qkv_body, float(GQA_HEAD) ** -0.5),
        grid=(lay.nblocks,),
        in_specs=[_rows(d), _full((1, d)), _modspec(lay), _full((d, n_q + 2 * n_kv)), _full((1, LANES)),
                  _full((1, LANES))] + [_posrows(lay)] * 3,
        out_specs=[_rows(n_q), _rows(n_kv), _rows(n_kv)],
        out_shape=[jax.ShapeDtypeStruct((lay.n, n_q), BF16), jax.ShapeDtypeStruct((lay.n, n_kv), BF16),
                   jax.ShapeDtypeStruct((lay.n, n_kv), BF16)],
        compiler_params=_cparams(("parallel",)),
        name="gqa_qkv",
    )(h, gain.reshape(1, d), mod, w_qkv.astype(BF16), q_norm.reshape(1, LANES), k_norm.reshape(1, LANES), *tabs)
    o = attention(lay, q, k, v, GQA_HEADS, GQA_HEADS // GQA_KV_HEADS, GQA_HEAD, GQA_HEAD)
    return out_proj(lay, o, w_o, jnp.zeros((d,), F32), h, mod)


def _ffn_prep_body(h_ref, g_ref, mod_ref, wr_ref, br_ref, v_ref, idx_ref, wt_ref):
    v = (_rms(h_ref[...], g_ref[...]) * (1.0 + mod_ref[0, 4:5, :]) + mod_ref[0, 3:4, :]).astype(BF16)
    v_ref[...] = v
    logit = jnp.dot(v, wr_ref[...], preferred_element_type=F32) + br_ref[...]
    lane = lax.broadcasted_iota(jnp.int32, logit.shape, 1).astype(F32)
    idx = jnp.zeros_like(logit)
    ex = jnp.zeros_like(logit)
    den = 0.0
    m0 = None
    for kx in range(TOP_K):
        m = jnp.max(logit, axis=-1, keepdims=True)
        am = jnp.min(jnp.where(logit == m, lane, float(LANES)), axis=-1, keepdims=True)
        if kx == 0:
            m0 = m
        e = jnp.exp(m - m0)
        den = den + e
        idx = jnp.where(lane == kx, am, idx)
        ex = jnp.where(lane == kx, e, ex)
        logit = jnp.where(lane == am, -jnp.inf, logit)
    idx_ref[...] = idx.astype(jnp.int32)
    wt_ref[...] = ex / den


def ffn_prep(lay, h, gain, mod, w_r, b_r):
    d = lay.d
    ne = w_r.shape[1]
    wr = jnp.pad(w_r, ((0, 0), (0, LANES - ne))).astype(BF16)
    br = jnp.concatenate([b_r, jnp.full((LANES - ne,), -1e30, F32)]).reshape(1, LANES)
    return pl.pallas_call(
        _ffn_prep_body,
        grid=(lay.nblocks,),
        in_specs=[_rows(d), _full((1, d)), _modspec(lay), _full((d, LANES)), _full((1, LANES))],
        out_specs=[_rows(d), _rows(LANES), _rows(LANES)],
        out_shape=[jax.ShapeDtypeStruct((lay.n, d), BF16), jax.ShapeDtypeStruct((lay.n, LANES), jnp.int32),
                   jax.ShapeDtypeStruct((lay.n, LANES), F32)],
        compiler_params=_cparams(("parallel",)),
        name="ffn_prep",
    )(h, gain.reshape(1, d), mod, wr, br)


def _experts_body(be_ref, bv_ref, x_ref, wg_ref, bg_ref, wu_ref, bu_ref, wd_ref, bd_ref, rw_ref, y_ref):
    i = pl.program_id(0)

    @pl.when(bv_ref[i] > 0)
    def _():
        x = x_ref[...]
        gt = jnp.minimum(jnp.dot(x, wg_ref[0], preferred_element_type=F32) + bg_ref[0], SWIGLU_LIMIT)
        up = jnp.clip(jnp.dot(x, wu_ref[0], preferred_element_type=F32) + bu_ref[0], -SWIGLU_LIMIT, SWIGLU_LIMIT)
        act = gt * _sigmoid(SWIGLU_ALPHA * gt) * (up + 1.0)
        y = _dot(act, wd_ref[0]) + bd_ref[0]
        y_ref[...] = y * rw_ref[...]

    @pl.when(bv_ref[i] == 0)
    def _():
        y_ref[...] = jnp.zeros_like(y_ref)


def experts(layer, xs, row_w, blk_e, blk_valid, wg, bg, wu, bu, wd, bd):
    n_rows, d = xs.shape
    nl, ne, _, ff = wg.shape
    r = EXPERT_ROWS
    grid_spec = pltpu.PrefetchScalarGridSpec(
        num_scalar_prefetch=2,
        grid=(n_rows // r,),
        in_specs=[pl.BlockSpec((r, d), lambda i, be, bv: (i, 0)),
                  pl.BlockSpec((None, 1, d, ff), lambda i, be, bv: (layer, be[i], 0, 0)),
                  pl.BlockSpec((None, 1, 1, ff), lambda i, be, bv: (layer, be[i], 0, 0)),
                  pl.BlockSpec((None, 1, d, ff), lambda i, be, bv: (layer, be[i], 0, 0)),
                  pl.BlockSpec((None, 1, 1, ff), lambda i, be, bv: (layer, be[i], 0, 0)),
                  pl.BlockSpec((None, 1, ff, d), lambda i, be, bv: (layer, be[i], 0, 0)),
                  pl.BlockSpec((None, 1, 1, d), lambda i, be, bv: (layer, be[i], 0, 0)),
                  pl.BlockSpec((r, 1), lambda i, be, bv: (i, 0))],
        out_specs=pl.BlockSpec((r, d), lambda i, be, bv: (i, 0)),
    )
    return pl.pallas_call(
        _experts_body,
        grid_spec=grid_spec,
        out_shape=jax.ShapeDtypeStruct((n_rows, d), F32),
        compiler_params=_cparams(("arbitrary",)),
        name="experts",
    )(blk_e, blk_valid, xs, wg, bg.reshape(nl, ne, 1, ff), wu, bu.reshape(nl, ne, 1, ff), wd,
      bd.reshape(nl, ne, 1, d), row_w.reshape(n_rows, 1))


def _ffn_out_body(h_ref, f0_ref, f1_ref, f2_ref, f3_ref, mod_ref, out_ref):
    f = (f0_ref[...] + f1_ref[...]) + (f2_ref[...] + f3_ref[...])
    out_ref[...] = h_ref[...] + mod_ref[0, 5:6, :] * f


def ffn_out(lay, h, fs, mod):
    d = lay.d
    return pl.pallas_call(
        _ffn_out_body,
        grid=(lay.nblocks,),
        in_specs=[_rows(d)] * (1 + TOP_K) + [_modspec(lay)],
        out_specs=_rows(d),
        out_shape=jax.ShapeDtypeStruct((lay.n, d), F32),
        compiler_params=_cparams(("parallel",)),
        name="ffn_out",
    )(h, *fs, mod)


def moe_layer(lay, layer, h, mod, gain, w_r, b_r, wg, bg, wu, bu, wd, bd):
    ne = w_r.shape[1]
    r = EXPERT_ROWS
    v, idx, wts = ffn_prep(lay, h, gain, mod, w_r, b_r)
    n_slot = lay.n * TOP_K
    slot_e = idx[:, :TOP_K].reshape(n_slot)
    slot_w = wts[:, :TOP_K].reshape(n_slot)
    slot_tok = jnp.repeat(jnp.arange(lay.n, dtype=jnp.int32), TOP_K)
    onehot = (slot_e[:, None] == jnp.arange(ne, dtype=jnp.int32)[None, :]).astype(jnp.int32)
    cum = jnp.cumsum(onehot, axis=0)
    rank = jnp.sum(cum * onehot, axis=1) - 1
    counts = cum[-1]
    padded = (counts + r - 1) // r * r
    pad_end = jnp.cumsum(padded)
    pad_start = pad_end - padded
    dest = pad_start[slot_e] + rank
    n_rows = n_slot + ne * r
    row_tok = jnp.zeros((n_rows,), jnp.int32).at[dest].set(slot_tok)
    row_w = jnp.zeros((n_rows,), F32).at[dest].set(slot_w)
    blk_start = jnp.arange(n_rows // r, dtype=jnp.int32) * r
    blk_e = jnp.minimum(jnp.searchsorted(pad_end, blk_start, side='right'), ne - 1).astype(jnp.int32)
    blk_valid = (blk_start < pad_end[-1]).astype(jnp.int32)
    xs = v.at[row_tok].get(mode="promise_in_bounds")
    y = experts(layer, xs, row_w, blk_e, blk_valid, wg, bg, wu, bu, wd, bd)
    dest4 = dest.reshape(lay.n, TOP_K)
    fs = [y.at[dest4[:, kx]].get(mode="promise_in_bounds") for kx in range(TOP_K)]
    return ffn_out(lay, h, fs, mod)


def kernel(x, c, ctx, c_ctx, mod_w, mod_b, norm_mix, norm_ffn, router_w, router_b, exp_w_gate, exp_b_gate, exp_w_up, exp_b_up, exp_w_down, exp_b_down, rw_mu, rw_wr, rw_wk, rw_wv, rw_wo, rw_w0, rw_w1, rw_w2, rw_a0, rw_a1, rw_a2, rw_g1, rw_g2, rw_kk, rw_ka, rw_rk, rw_lnx_g, rw_lnx_b, mla_w_down, mla_qa_norm, mla_w_qb, mla_kv_norm, mla_w_kvb, mla_qn_nope, mla_qn_rope, mla_kn_nope, mla_kn_rope, mla_wo, fn_w, fn_b, gqa_w_qkv, gqa_q_norm, gqa_k_norm, gqa_wo):
    bsz, seq, d = x.shape
    ctx_len = ctx.shape[1]
    depth = mod_w.shape[0]
    lay = Layout(bsz, ctx_len, seq, d)
    assert bsz < 16 and d // PAIR * PAIR == d

    cin = jnp.zeros((16, d), F32).at[:bsz].set(c).at[bsz].set(c_ctx)
    mod_all = modulation(cin, mod_w, mod_b).reshape(depth, 16, 6, d)

    h = jnp.concatenate([ctx, x], axis=1).reshape(lay.n, d)
    wg_all, wu_all, wd_all = exp_w_gate.astype(BF16), exp_w_up.astype(BF16), exp_w_down.astype(BF16)
    n_mixers = 4
    for i in range(depth):
        m, j = i % n_mixers, i // n_mixers
        mod = mod_all[i]
        if m == 0:
            h = rwkv_layer(lay, h, mod, norm_mix[i], rw_mu[j], rw_wr[j], rw_wk[j], rw_wv[j], rw_wo[j], rw_w0[j],
                           rw_w1[j], rw_w2[j], rw_a0[j], rw_a1[j], rw_a2[j], rw_g1[j], rw_g2[j], rw_kk[j],
                           rw_ka[j], rw_rk[j], rw_lnx_g[j], rw_lnx_b[j])
        elif m == 1:
            h = mla_layer(lay, h, mod, norm_mix[i], mla_w_down[j], mla_qa_norm[j], mla_w_qb[j], mla_kv_norm[j],
                          mla_w_kvb[j], mla_qn_nope[j], mla_qn_rope[j], mla_kn_nope[j], mla_kn_rope[j], mla_wo[j])
        elif m == 2:
            h = fourier_layer(lay, h, mod, norm_mix[i], fn_w[j], fn_b[j])
        else:
            h = gqa_layer(lay, h, mod, norm_mix[i], gqa_w_qkv[j], gqa_q_norm[j], gqa_k_norm[j], gqa_wo[j])
        h = moe_layer(lay, i, h, mod, norm_ffn[i], router_w[i], router_b[i], wg_all, exp_b_gate, wu_all, exp_b_up,
                      wd_all, exp_b_down)
    return h.reshape(bsz, lay.tt, d)[:, ctx_len:, :]
```

```python
import functools
import math

import numpy as np
import jax
import jax.numpy as jnp
from jax import lax
from jax.experimental import pallas as pl
from jax.experimental.pallas import tpu as pltpu

F32 = jnp.float32
BF16 = jnp.bfloat16

LANES = 128
SUBLANES = 8
VMEM_LIMIT_BYTES = 56 * 1024 * 1024

GRID_W = 64
ROPE_THETA = 10000.0
NORM_EPS = 1e-6
RW_HEAD = 64
RW_GN_EPS = 64e-5
MLA_HEADS = 16
MLA_Q_RANK = 512
MLA_KV_RANK = 512
MLA_NOPE = 128
MLA_ROPE = 64
MLA_V = 128
FN_GROUPS = 8
GQA_HEADS = 16
GQA_KV_HEADS = 4
GQA_HEAD = 128
TOP_K = 4
SWIGLU_LIMIT = 7.0
SWIGLU_ALPHA = 1.702

TM = 256
WKV_CHUNK = 64
WKV_GROUP = 16
ATTN_ROWS = 256
EXPERT_ROWS = 256
PAIR = 2 * RW_HEAD


def _cparams(sem):
    return pltpu.CompilerParams(dimension_semantics=sem, vmem_limit_bytes=VMEM_LIMIT_BYTES)


def _dot(a, b):
    return jnp.dot(a.astype(BF16), b.astype(BF16), preferred_element_type=F32)


def _dot_nt(a, b):
    return lax.dot_general(a.astype(BF16), b.astype(BF16), (((1,), (1,)), ((), ())),
                           preferred_element_type=F32)


def _rms(x, gain):
    return x * lax.rsqrt(jnp.mean(x * x, axis=-1, keepdims=True) + NORM_EPS) * gain


def _sigmoid(x):
    return 1.0 / (1.0 + jnp.exp(-x))


class Layout:
    def __init__(self, batch, ctx_len, seq, d):
        self.b, self.ctx, self.seq, self.d = batch, ctx_len, seq, d
        self.tt = ctx_len + seq
        self.n = batch * self.tt
        assert ctx_len % TM == 0 and seq % TM == 0
        self.nb = self.tt // TM
        self.cb = ctx_len // TM
        self.nblocks = self.n // TM

    def seg(self, i):
        return jnp.where(i % self.nb < self.cb, self.b, i // self.nb)


def _rows(ncols, tm=TM):
    return pl.BlockSpec((tm, ncols), lambda i: (i, 0))


def _full(shape):
    nd = len(shape)
    return pl.BlockSpec(shape, lambda i: (0,) * nd)


def _modspec(lay):
    return pl.BlockSpec((1, 6, lay.d), lambda i: (lay.seg(i), 0, 0))


def _posrows(lay, ncols=LANES):
    return pl.BlockSpec((TM, ncols), lambda i: (i % lay.nb, 0))


def _pairs(tm=TM):
    return pl.BlockSpec((None, tm, PAIR), lambda i: (0, i, 0))


def _mod_body(c_ref, w_ref, b_ref, o_ref):
    c = c_ref[...]
    s = c * _sigmoid(c)
    o_ref[0] = _dot(s, w_ref[0]) + b_ref[0]


def modulation(cin, mod_w, mod_b):
    depth, d, n6 = mod_w.shape
    tn = 1024
    return pl.pallas_call(
        _mod_body,
        grid=(depth, n6 // tn),
        in_specs=[pl.BlockSpec((16, d), lambda l, j: (0, 0)),
                  pl.BlockSpec((1, d, tn), lambda l, j: (l, 0, j)),
                  pl.BlockSpec((1, 1, tn), lambda l, j: (l, 0, j))],
        out_specs=pl.BlockSpec((1, 16, tn), lambda l, j: (l, 0, j)),
        out_shape=jax.ShapeDtypeStruct((depth, 16, n6), F32),
        compiler_params=_cparams(("parallel", "parallel")),
        name="modulation",
    )(cin, mod_w, mod_b.reshape(depth, 1, n6))


def _rw_prep_body(lay, h_ref, hp_ref, hn_ref, g_ref, mod_ref, mu_ref, *outs):
    j = pl.program_id(0) % lay.nb
    seg_start = jnp.logical_or(j == 0, j == lay.cb)
    seg_end = jnp.logical_or(j == lay.cb - 1, j == lay.nb - 1)
    gain = g_ref[...]
    shift, scale = mod_ref[0, 0:1, :], mod_ref[0, 1:2, :]

    def norm_mod(x):
        return _rms(x, gain) * (1.0 + scale) + shift

    u = norm_mod(h_ref[...])
    prev = jnp.where(seg_start, 0.0, norm_mod(hp_ref[SUBLANES - 1:SUBLANES, :]))
    nxt = jnp.where(seg_end, 0.0, norm_mod(hn_ref[0:1, :]))
    row = lax.broadcasted_iota(jnp.int32, u.shape, 0)
    up = jnp.where(row == 0, prev, pltpu.roll(u, 1, 0))
    un = jnp.where(row == TM - 1, nxt, pltpu.roll(u, TM - 1, 0))
    du = 0.5 * (up + un) - u
    for n, o_ref in enumerate(outs):
        o_ref[...] = (u + du * mu_ref[n:n + 1, :]).astype(BF16)


def rw_prep(lay, h, gain, mod, mu):
    d = lay.d
    r8 = TM // SUBLANES
    last8 = lay.n // SUBLANES - 1
    mu8 = jnp.concatenate([mu, jnp.zeros((2, d), F32)], axis=0)
    return pl.pallas_call(
        functools.partial(_rw_prep_body, lay),
        grid=(lay.nblocks,),
        in_specs=[_rows(d),
                  pl.BlockSpec((SUBLANES, d), lambda i: (jnp.maximum(i * r8 - 1, 0), 0)),
                  pl.BlockSpec((SUBLANES, d), lambda i: (jnp.minimum((i + 1) * r8, last8), 0)),
                  _full((1, d)), _modspec(lay), _full((8, d))],
        out_specs=[_rows(d)] * 6,
        out_shape=[jax.ShapeDtypeStruct((lay.n, d), BF16)] * 6,
        compiler_params=_cparams(("parallel",)),
        name="rw_prep",
    )(h, h, h, gain.reshape(1, d), mod, mu8)


def _linear_pairs_body(npairs, x_ref, w_ref, o_ref):
    y = _dot(x_ref[...], w_ref[...])
    for p in range(npairs):
        o_ref[p] = y[:, p * PAIR:(p + 1) * PAIR]


def linear_pairs(lay, x, w):
    d = lay.d
    npairs = d // PAIR
    return pl.pallas_call(
        functools.partial(_linear_pairs_body, npairs),
        grid=(lay.nblocks,),
        in_specs=[_rows(d), _full((d, d))],
        out_specs=pl.BlockSpec((npairs, TM, PAIR), lambda i: (0, i, 0)),
        out_shape=jax.ShapeDtypeStruct((npairs, lay.n, PAIR), F32),
        compiler_params=_cparams(("parallel",)),
        name="rw_linear",
    )(x, w)


def _rw_lora_body(npairs, xw_ref, xa_ref, xg_ref, w1_ref, w2_ref, a1_ref, a2_ref, g1_ref, g2_ref,
                  w0_ref, a0_ref, lw_ref, ag_ref, g_ref):
    hw = jnp.tanh(_dot(xw_ref[...], w1_ref[...]))
    ha = _dot(xa_ref[...], a1_ref[...])
    hg = _sigmoid(_dot(xg_ref[...], g1_ref[...]))
    rw = w1_ref.shape[1] // 2
    rg = g1_ref.shape[1] // 2
    for dr in range(2):
        z = w0_ref[dr:dr + 1, :] + _dot(hw[:, dr * rw:(dr + 1) * rw], w2_ref[dr])
        lw = -math.exp(-0.5) * _sigmoid(z)
        a = _sigmoid(a0_ref[dr:dr + 1, :] + _dot(ha[:, dr * rw:(dr + 1) * rw], a2_ref[dr]))
        g = _dot(hg[:, dr * rg:(dr + 1) * rg], g2_ref[dr])
        for p in range(npairs):
            sl = slice(p * PAIR, (p + 1) * PAIR)
            lw_ref[dr, p] = lw[:, sl]
            ag_ref[dr, p] = a[:, sl]
            g_ref[dr, p] = g[:, sl]


def _pad_lora(w1, w2):
    r = w1.shape[2]
    rp = -(-r // LANES) * LANES
    w1p = jnp.pad(w1, ((0, 0), (0, 0), (0, rp - r)))
    w1p = jnp.concatenate([w1p[0], w1p[1]], axis=1).astype(BF16)
    w2p = jnp.pad(w2, ((0, 0), (0, rp - r), (0, 0))).astype(BF16)
    return w1p, w2p


def rw_lora(lay, xw, xa, xg, w0, w1, w2, a0, a1, a2, g1, g2):
    d = lay.d
    npairs = d // PAIR
    w1p, w2p = _pad_lora(w1, w2)
    a1p, a2p = _pad_lora(a1, a2)
    g1p, g2p = _pad_lora(g1, g2)
    ospec = pl.BlockSpec((2, npairs, TM, PAIR), lambda i: (0, 0, i, 0))
    oshape = jax.ShapeDtypeStruct((2, npairs, lay.n, PAIR), F32)
    return pl.pallas_call(
        functools.partial(_rw_lora_body, npairs),
        grid=(lay.nblocks,),
        in_specs=[_rows(d)] * 3 + [_full(w1p.shape), _full(w2p.shape), _full(a1p.shape), _full(a2p.shape),
                                    _full(g1p.shape), _full(g2p.shape), _full((2, d)), _full((2, d))],
        out_specs=[ospec] * 3,
        out_shape=[oshape] * 3,
        compiler_params=_cparams(("parallel",)),
        name="rw_lora",
    )(xw, xa, xg, w1p, w2p, a1p, a2p, g1p, g2p, w0, a0)


def _bd(z, head0):
    return jnp.concatenate([jnp.where(head0, z, 0.0), jnp.where(head0, 0.0, z)], axis=0)


def _head_sum(x, head0):
    s0 = jnp.sum(jnp.where(head0, x, 0.0), axis=-1, keepdims=True)
    s1 = jnp.sum(jnp.where(head0, 0.0, x), axis=-1, keepdims=True)
    return jnp.where(head0, s0, s1)


def _wkv_group_chunk(rev, tiles, par, hts):
    c = WKV_CHUNK
    rng = range(len(tiles))
    t = lax.broadcasted_iota(jnp.int32, (c, PAIR), 0)
    lane = lax.broadcasted_iota(jnp.int32, (c, PAIR), 1)
    s = lane % RW_HEAD
    head0 = lane < RW_HEAD
    strict = (s > t) if rev else (s < t)
    incl = (s >= t) if rev else (s <= t)
    eye = jnp.where(s == t, 1.0, 0.0)
    rr = lax.broadcasted_iota(jnp.int32, (PAIR, PAIR), 0) // RW_HEAD
    cc = lax.broadcasted_iota(jnp.int32, (PAIR, PAIR), 1) // RW_HEAD
    same_head = rr == cc

    def prep(q):
        r, k, v, lw, ag, _ = tiles[q]
        kk_p, ka_p = par[q][0], par[q][1]
        kkf = k * kk_p
        kk = kkf / jnp.maximum(jnp.sqrt(_head_sum(kkf * kkf, head0)), 1e-12)
        kd = k * (1.0 + (ag - 1.0) * ka_p)
        bvec = kk * ag
        cs = lw
        for sh in (1, 2, 4, 8, 16, 32):
            if rev:
                cs = cs + jnp.where(t + sh < c, pltpu.roll(cs, c - sh, 0), 0.0)
            else:
                cs = cs + jnp.where(t >= sh, pltpu.roll(cs, sh, 0), 0.0)
        cl = cs[0:1, :] if rev else cs[c - 1:c, :]
        e_neg = jnp.exp(-cs)
        e_end = jnp.exp(cl - cs)
        ar = jnp.concatenate([-kk * jnp.exp(cs - lw), r * jnp.exp(cs)], axis=0).astype(BF16)
        bk = jnp.concatenate([_bd(bvec * e_neg, head0), _bd(kd * e_neg, head0)], axis=0).astype(BF16)
        bk_end = jnp.concatenate([bvec * e_end, kd * e_end], axis=0).astype(BF16)
        return dict(ar=ar, bk=bk, bk_end=bk_end, kd=kd, decay=jnp.exp(cl), vbd=_bd(v, head0).astype(BF16))

    st = [prep(q) for q in rng]
    p = [_dot_nt(st[q]["ar"], st[q]["bk"]) for q in rng]
    arh = [_dot_nt(st[q]["ar"], hts[q]) for q in rng]
    l_ab = [jnp.where(strict, p[q][0:c, 0:PAIR], 0.0) for q in rng]
    l_ak = [jnp.where(strict, p[q][0:c, PAIR:2 * PAIR], 0.0) for q in rng]
    m_r = [jnp.concatenate([jnp.where(incl, p[q][c:2 * c, 0:PAIR], 0.0),
                            jnp.where(incl, p[q][c:2 * c, PAIR:2 * PAIR], 0.0)], axis=1).astype(BF16) for q in rng]
    x = [arh[q][0:c] + _dot(l_ak[q], st[q]["vbd"]) for q in rng]
    tinv = [eye + l_ab[q] for q in rng]
    pw = l_ab
    for _ in range(5):
        pw = [_dot(pw[q], _bd(pw[q], head0)) for q in rng]
        tinv = [tinv[q] + _dot(tinv[q], _bd(pw[q], head0)) for q in rng]
    u = [_dot(tinv[q], _bd(x[q], head0)) for q in rng]
    y = [arh[q][c:2 * c] + _dot(m_r[q], jnp.concatenate([_bd(u[q], head0).astype(BF16), st[q]["vbd"]], axis=0))
         for q in rng]
    upd = [_dot(jnp.concatenate([u[q], tiles[q][2]], axis=0).T, st[q]["bk_end"]) for q in rng]
    ht_new = [hts[q] * st[q]["decay"] + jnp.where(same_head, upd[q], 0.0) for q in rng]

    outs = []
    inv_n = 1.0 / RW_HEAD
    for q in rng:
        r, _, v, _, _, g = tiles[q]
        rk_p, lg_p, lb_p = par[q][2], par[q][3], par[q][4]
        mu = _head_sum(y[q], head0) * inv_n
        yc = y[q] - mu
        var = _head_sum(yc * yc, head0) * inv_n
        bonus = _head_sum(r * st[q]["kd"] * rk_p, head0) * v
        outs.append((yc * lax.rsqrt(var + RW_GN_EPS) * lg_p + lb_p + bonus) * g)
    return outs, ht_new


def _wkv_body(rev, npairs, r_ref, k_ref, v_ref, lw_ref, ag_ref, g_ref, kk_ref, ka_ref, rk_ref, lg_ref, lb_ref,
              o_ref, ht_ref):
    @pl.when(pl.program_id(1) == 0)
    def _():
        ht_ref[...] = jnp.zeros_like(ht_ref)

    def group(gi, carry):
        ps = [gi * WKV_GROUP + q for q in range(WKV_GROUP)]
        tiles = [(r_ref[p], k_ref[p], v_ref[p], lw_ref[0, p], ag_ref[0, p], g_ref[0, p]) for p in ps]
        par = [(kk_ref[p], ka_ref[p], rk_ref[p], lg_ref[p], lb_ref[p]) for p in ps]
        outs, hts = _wkv_group_chunk(rev, tiles, par, [ht_ref[p] for p in ps])
        for q, p in enumerate(ps):
            o_ref[p] = outs[q]
            ht_ref[p] = hts[q]
        return carry

    lax.fori_loop(0, npairs // WKV_GROUP, group, 0)


def wkv(lay, rev, r, k, v, lw, ag, g, params):
    npairs = lay.d // PAIR
    c = WKV_CHUNK
    nch = lay.tt // c
    cch = lay.ctx // c
    dr = 1 if rev else 0

    def rowblk(b, ci):
        if rev:
            return b * nch + jnp.where(ci < cch, cch - 1 - ci, nch + cch - 1 - ci)
        return b * nch + ci

    tok = pl.BlockSpec((npairs, c, PAIR), lambda b, ci: (0, rowblk(b, ci), 0))
    tokd = pl.BlockSpec((1, npairs, c, PAIR), lambda b, ci: (dr, 0, rowblk(b, ci), 0))
    par = pl.BlockSpec((npairs, 1, PAIR), lambda b, ci: (0, 0, 0))
    return pl.pallas_call(
        functools.partial(_wkv_body, rev, npairs),
        grid=(lay.b, nch),
        in_specs=[tok, tok, tok, tokd, tokd, tokd] + [par] * 5,
        out_specs=tok,
        out_shape=jax.ShapeDtypeStruct((npairs, lay.n, PAIR), F32),
        scratch_shapes=[pltpu.VMEM((npairs, PAIR, PAIR), F32)],
        compiler_params=_cparams(("parallel", "arbitrary")),
        name="wkv_rev" if rev else "wkv_fwd",
    )(r, k, v, lw, ag, g, *params)


def _rw_out_body(npairs, o0_ref, o1_ref, h_ref, mod_ref, w_ref, out_ref):
    acc = jnp.concatenate([o0_ref[p] + o1_ref[p] for p in range(npairs)], axis=-1)
    out_ref[...] = h_ref[...] + mod_ref[0, 2:3, :] * _dot(acc, w_ref[...])


def rw_out(lay, o0, o1, h, mod, w):
    d = lay.d
    npairs = d // PAIR
    pm = pl.BlockSpec((npairs, TM, PAIR), lambda i: (0, i, 0))
    return pl.pallas_call(
        functools.partial(_rw_out_body, npairs),
        grid=(lay.nblocks,),
        in_specs=[pm, pm, _rows(d), _modspec(lay), _full((d, d))],
        out_specs=_rows(d),
        out_shape=jax.ShapeDtypeStruct((lay.n, d), F32),
        compiler_params=_cparams(("parallel",)),
        name="rw_out",
    )(o0, o1, h, mod, w)


def rwkv_layer(lay, h, mod, gain, mu, w_r, w_k, w_v, w_o, w0, w1, w2, a0, a1, a2, g1, g2, k_k, k_a, r_k,
               lnx_g, lnx_b):
    npairs = lay.d // PAIR
    xr, xw, xk, xv, xa, xg = rw_prep(lay, h, gain, mod, mu)
    r = linear_pairs(lay, xr, w_r.astype(BF16))
    k = linear_pairs(lay, xk, w_k.astype(BF16))
    v = linear_pairs(lay, xv, w_v.astype(BF16))
    lw, ag, g = rw_lora(lay, xw, xa, xg, w0, w1, w2, a0, a1, a2, g1, g2)
    params = [t.reshape(npairs, 1, PAIR) for t in (k_k, k_a, r_k.reshape(-1), lnx_g, lnx_b)]
    o0 = wkv(lay, False, r, k, v, lw, ag, g, params)
    o1 = wkv(lay, True, r, k, v, lw, ag, g, params)
    return rw_out(lay, o0, o1, h, mod, w_o.astype(BF16))


def rope_tables(lay, rot_dim):
    quarter = rot_dim // 4
    half = rot_dim // 2
    t = np.arange(lay.seq)
    inv = ROPE_THETA ** (-np.arange(quarter, dtype=np.float32) / quarter)
    ang = np.concatenate([(t // GRID_W)[:, None].astype(np.float32) * inv,
                          (t % GRID_W)[:, None].astype(np.float32) * inv], axis=-1)
    ang = jnp.asarray(ang, F32)
    cos, sin = jnp.cos(ang), jnp.sin(ang)
    pad = LANES - rot_dim
    zer = jnp.zeros((lay.seq, half), F32)
    c = jnp.concatenate([cos, cos, jnp.ones((lay.seq, pad), F32)], axis=-1)
    s_lo = jnp.concatenate([-sin, zer, jnp.zeros((lay.seq, pad), F32)], axis=-1)
    s_hi = jnp.concatenate([zer, sin, jnp.zeros((lay.seq, pad), F32)], axis=-1)

    def with_ctx(tab, fill):
        return jnp.concatenate([jnp.full((lay.ctx, LANES), fill, F32), tab], axis=0)

    return with_ctx(c, 1.0), with_ctx(s_lo, 0.0), with_ctx(s_hi, 0.0)


def _rope(x, c, s_lo, s_hi, half):
    return x * c + pltpu.roll(x, LANES - half, 1) * s_lo + pltpu.roll(x, half, 1) * s_hi


def _attn_rows(ctx, rc, row0, q_ref, k, v, o_ref):
    n = q_ref.shape[0] // rc
    ss = []
    for c in range(n):
        s = _dot_nt(q_ref[c * rc:(c + 1) * rc, :], k)
        if row0 is not None and row0 + c * rc < ctx:
            qrow = row0 + c * rc + lax.broadcasted_iota(jnp.int32, s.shape, 0)
            kcol = lax.broadcasted_iota(jnp.int32, s.shape, 1)
            s = jnp.where(jnp.logical_or(qrow >= ctx, kcol < ctx), s, -1e30)
        ss.append(s)
    ms = [jnp.max(s, axis=-1, keepdims=True) for s in ss]
    ps = [jnp.exp(ss[c] - ms[c]) for c in range(n)]
    ls = [jnp.sum(p, axis=-1, keepdims=True) for p in ps]
    for c in range(n):
        o_ref[c * rc:(c + 1) * rc, :] = (_dot(ps[c], v) / ls[c]).astype(o_ref.dtype)


def _attn_body(ctx, q_ref, k_ref, v_ref, o_ref):
    k = k_ref[...]
    v = v_ref[...]

    @pl.when(pl.program_id(2) == 0)
    def _():
        _attn_rows(ctx, ATTN_ROWS, 0, q_ref, k, v, o_ref)

    @pl.when(pl.program_id(2) != 0)
    def _():
        _attn_rows(ctx, ATTN_ROWS, None, q_ref, k, v, o_ref)


def attention(lay, q, k, v, n_heads, kv_group, dq, dv, tq=768):
    nq = lay.tt // tq
    assert lay.ctx <= tq and tq % ATTN_ROWS == 0
    k3 = k.reshape(lay.b, lay.tt, k.shape[1])
    v3 = v.reshape(lay.b, lay.tt, v.shape[1])
    return pl.pallas_call(
        functools.partial(_attn_body, lay.ctx),
        grid=(lay.b, n_heads, nq),
        in_specs=[pl.BlockSpec((tq, dq), lambda b, h, i: (b * nq + i, h)),
                  pl.BlockSpec((None, lay.tt, dq), lambda b, h, i: (b, 0, h // kv_group)),
                  pl.BlockSpec((None, lay.tt, dv), lambda b, h, i: (b, 0, h // kv_group))],
        out_specs=pl.BlockSpec((tq, dv), lambda b, h, i: (b * nq + i, h)),
        out_shape=jax.ShapeDtypeStruct((lay.n, n_heads * dv), BF16),
        compiler_params=_cparams(("parallel", "parallel", "arbitrary")),
        name="attention",
    )(q, k3, v3)


def _out_proj_body(x_ref, w_ref, b_ref, h_ref, mod_ref, out_ref):
    y = _dot(x_ref[...], w_ref[...]) + b_ref[...]
    out_ref[...] = h_ref[...] + mod_ref[0, 2:3, :] * y


def out_proj(lay, x, w, bias, h, mod):
    d = lay.d
    kdim = x.shape[1]
    return pl.pallas_call(
        _out_proj_body,
        grid=(lay.nblocks,),
        in_specs=[_rows(kdim), _full((kdim, d)), _full((1, d)), _rows(d), _modspec(lay)],
        out_specs=_rows(d),
        out_shape=jax.ShapeDtypeStruct((lay.n, d), F32),
        compiler_params=_cparams(("parallel",)),
        name="out_proj",
    )(x, w.astype(BF16), bias.reshape(1, d), h, mod)


def _mla_down_body(h_ref, g_ref, mod_ref, w_ref, qan_ref, kvn_ref, krn_ref, c_ref, slo_ref, shi_ref,
                   q_ref, ckv_ref, kr_ref):
    u = _rms(h_ref[...], g_ref[...]) * (1.0 + mod_ref[0, 1:2, :]) + mod_ref[0, 0:1, :]
    down = _dot(u, w_ref[...])
    q_ref[...] = _rms(down[:, 0:MLA_Q_RANK], qan_ref[...]).astype(BF16)
    ckv_ref[...] = _rms(down[:, MLA_Q_RANK:MLA_Q_RANK + MLA_KV_RANK], kvn_ref[...]).astype(BF16)
    kr = down[:, MLA_Q_RANK + MLA_KV_RANK:]
    ms = jnp.sum(kr * kr, axis=-1, keepdims=True) * (1.0 / MLA_ROPE)
    kr = kr * lax.rsqrt(ms + NORM_EPS) * krn_ref[...]
    kr_ref[...] = _rope(kr, c_ref[...], slo_ref[...], shi_ref[...], MLA_ROPE // 2).astype(BF16)


def _mla_q_body(scale, x_ref, w_ref, nn_ref, rn_ref, c_ref, slo_ref, shi_ref, q_ref):
    q = _dot(x_ref[...], w_ref[...])
    hw = 2 * LANES
    for hd in range(MLA_HEADS):
        nope = q[:, hd * hw:hd * hw + LANES]
        rope = q[:, hd * hw + LANES:(hd + 1) * hw]
        nope = _rms(nope, nn_ref[...])
        ms = jnp.sum(rope * rope, axis=-1, keepdims=True) * (1.0 / MLA_ROPE)
        rope = rope * lax.rsqrt(ms + NORM_EPS) * rn_ref[...]
        rope = _rope(rope, c_ref[...], slo_ref[...], shi_ref[...], MLA_ROPE // 2)
        q_ref[:, hd * hw:hd * hw + LANES] = (nope * scale).astype(BF16)
        q_ref[:, hd * hw + LANES:(hd + 1) * hw] = (rope * scale).astype(BF16)


def _mla_kv_body(x_ref, w_ref, kn_ref, kr_ref, k_ref, v_ref):
    kvb = _dot(x_ref[...], w_ref[...])
    hw = 2 * LANES
    kr = kr_ref[...]
    for hd in range(MLA_HEADS):
        k_ref[:, hd * hw:hd * hw + LANES] = _rms(kvb[:, hd * LANES:(hd + 1) * LANES], kn_ref[...]).astype(BF16)
        k_ref[:, hd * hw + LANES:(hd + 1) * hw] = kr
    v_ref[...] = kvb[:, MLA_HEADS * LANES:].astype(BF16)


def mla_layer(lay, h, mod, gain, w_down, qa_norm, w_qb, kv_norm, w_kvb, qn_nope, qn_rope, kn_nope, kn_rope, w_o):
    d = lay.d
    tabs = rope_tables(lay, MLA_ROPE)
    pad64 = lambda t: jnp.concatenate([t, jnp.zeros((LANES - MLA_ROPE,), F32)]).reshape(1, LANES)
    wd = jnp.pad(w_down, ((0, 0), (0, LANES - MLA_ROPE))).astype(BF16)
    nd = wd.shape[1]
    q_lat, ckv, kr = pl.pallas_call(
        _mla_down_body,
        grid=(lay.nblocks,),
        in_specs=[_rows(d), _full((1, d)), _modspec(lay), _full((d, nd)), _full((1, MLA_Q_RANK)),
                  _full((1, MLA_KV_RANK)), _full((1, LANES))] + [_posrows(lay)] * 3,
        out_specs=[_rows(MLA_Q_RANK), _rows(MLA_KV_RANK), _rows(LANES)],
        out_shape=[jax.ShapeDtypeStruct((lay.n, MLA_Q_RANK), BF16), jax.ShapeDtypeStruct((lay.n, MLA_KV_RANK), BF16),
                   jax.ShapeDtypeStruct((lay.n, LANES), BF16)],
        compiler_params=_cparams(("parallel",)),
        name="mla_down",
    )(h, gain.reshape(1, d), mod, wd, qa_norm.reshape(1, -1), kv_norm.reshape(1, -1), pad64(kn_rope), *tabs)

    hq = MLA_NOPE + MLA_ROPE
    wq = w_qb.reshape(MLA_Q_RANK, MLA_HEADS, hq)
    wq = jnp.pad(wq, ((0, 0), (0, 0), (0, 2 * LANES - hq))).reshape(MLA_Q_RANK, MLA_HEADS * 2 * LANES).astype(BF16)
    dq = 2 * LANES
    scale = float(hq) ** -0.5
    q = pl.pallas_call(
        functools.partial(_mla_q_body, scale),
        grid=(lay.nblocks,),
        in_specs=[_rows(MLA_Q_RANK), _full(wq.shape), _full((1, LANES)), _full((1, LANES))] + [_posrows(lay)] * 3,
        out_specs=_rows(MLA_HEADS * dq),
        out_shape=jax.ShapeDtypeStruct((lay.n, MLA_HEADS * dq), BF16),
        compiler_params=_cparams(("parallel",)),
        name="mla_q",
    )(q_lat, wq, qn_nope.reshape(1, LANES), pad64(qn_rope), *tabs)

    wkv_ = w_kvb.reshape(MLA_KV_RANK, MLA_HEADS, MLA_NOPE + MLA_V)
    wkv_ = jnp.concatenate([wkv_[:, :, :MLA_NOPE].reshape(MLA_KV_RANK, -1),
                            wkv_[:, :, MLA_NOPE:].reshape(MLA_KV_RANK, -1)], axis=1).astype(BF16)
    k, v = pl.pallas_call(
        _mla_kv_body,
        grid=(lay.nblocks,),
        in_specs=[_rows(MLA_KV_RANK), _full(wkv_.shape), _full((1, LANES)), _rows(LANES)],
        out_specs=[_rows(MLA_HEADS * dq), _rows(MLA_HEADS * MLA_V)],
        out_shape=[jax.ShapeDtypeStruct((lay.n, MLA_HEADS * dq), BF16),
                   jax.ShapeDtypeStruct((lay.n, MLA_HEADS * MLA_V), BF16)],
        compiler_params=_cparams(("parallel",)),
        name="mla_kv",
    )(ckv, wkv_, kn_nope.reshape(1, LANES), kr)

    o = attention(lay, q, k, v, MLA_HEADS, 1, dq, MLA_V)
    return out_proj(lay, o, w_o, jnp.zeros((d,), F32), h, mod)


def _dft_cos_sin(n):
    j = np.arange(n, dtype=np.int64)
    m = (j[:, None] * j[None, :]) % n
    ang = 2.0 * np.pi * m.astype(np.float64) / n
    return np.cos(ang), np.sin(ang)


def _fn_stage1_body(gw, h_ref, g_ref, mod_ref, cs_ref, zc_ref, zs_ref):
    u = (_rms(h_ref[...], g_ref[...]) * (1.0 + mod_ref[0, 1:2, :]) + mod_ref[0, 0:1, :]).astype(BF16)
    for gi in range(FN_GROUPS):
        z = jnp.dot(u[:, gi * gw:(gi + 1) * gw], cs_ref[...], preferred_element_type=F32)
        zc_ref[:, gi * gw:(gi + 1) * gw] = z[:, 0:gw].astype(BF16)
        zs_ref[:, gi * gw:(gi + 1) * gw] = z[:, gw:2 * gw].astype(BF16)


def _fn_stage2_body(lay, ct_ref, st_ref, zc_ref, zs_ref, o_ref):
    j = pl.program_id(1)

    def mix(lo, hi):
        f = (jnp.dot(ct_ref[:, lo:hi], zc_ref[lo:hi, :], preferred_element_type=F32)
             - jnp.dot(st_ref[:, lo:hi], zs_ref[lo:hi, :], preferred_element_type=F32))
        o_ref[...] = f.astype(BF16)

    @pl.when(j < lay.cb)
    def _():
        mix(0, lay.ctx)

    @pl.when(j >= lay.cb)
    def _():
        mix(lay.ctx, lay.tt)


def fourier_layer(lay, h, mod, gain, w, b):
    d = lay.d
    gw = d // FN_GROUPS
    cw, sw = _dft_cos_sin(gw)
    csw = jnp.asarray(np.concatenate([cw, sw], axis=1), F32).astype(BF16)
    zc, zs = pl.pallas_call(
        functools.partial(_fn_stage1_body, gw),
        grid=(lay.nblocks,),
        in_specs=[_rows(d), _full((1, d)), _modspec(lay), _full((gw, 2 * gw))],
        out_specs=[_rows(d), _rows(d)],
        out_shape=[jax.ShapeDtypeStruct((lay.n, d), BF16)] * 2,
        compiler_params=_cparams(("parallel",)),
        name="fn_stage1",
    )(h, gain.reshape(1, d), mod, csw)

    ct = np.zeros((lay.tt, lay.tt), np.float64)
    st = np.zeros((lay.tt, lay.tt), np.float64)
    for lo, n in ((0, lay.ctx), (lay.ctx, lay.seq)):
        c, s = _dft_cos_sin(n)
        sc = 1.0 / math.sqrt(n * gw)
        ct[lo:lo + n, lo:lo + n] = c * sc
        st[lo:lo + n, lo:lo + n] = s * sc
    ct = jnp.asarray(ct, F32).astype(BF16)
    st = jnp.asarray(st, F32).astype(BF16)
    zc3 = zc.reshape(lay.b, lay.tt, d)
    zs3 = zs.reshape(lay.b, lay.tt, d)
    f = pl.pallas_call(
        functools.partial(_fn_stage2_body, lay),
        grid=(lay.b, lay.nb),
        in_specs=[pl.BlockSpec((TM, lay.tt), lambda bb, j: (j, 0)),
                  pl.BlockSpec((TM, lay.tt), lambda bb, j: (j, 0)),
                  pl.BlockSpec((None, lay.tt, d), lambda bb, j: (bb, 0, 0)),
                  pl.BlockSpec((None, lay.tt, d), lambda bb, j: (bb, 0, 0))],
        out_specs=pl.BlockSpec((TM, d), lambda bb, j: (bb * lay.nb + j, 0)),
        out_shape=jax.ShapeDtypeStruct((lay.n, d), BF16),
        compiler_params=_cparams(("parallel", "arbitrary")),
        name="fn_stage2",
    )(ct, st, zc3, zs3)
    return out_proj(lay, f, w, b, h, mod)


def _gqa_qkv_body(scale, h_ref, g_ref, mod_ref, w_ref, qn_ref, kn_ref, c_ref, slo_ref, shi_ref,
                  q_ref, k_ref, v_ref):
    u = _rms(h_ref[...], g_ref[...]) * (1.0 + mod_ref[0, 1:2, :]) + mod_ref[0, 0:1, :]
    qkv = _dot(u, w_ref[...])
    c, slo, shi = c_ref[...], slo_ref[...], shi_ref[...]
    n_q = GQA_HEADS * GQA_HEAD
    n_kv = GQA_KV_HEADS * GQA_HEAD
    for hd in range(GQA_HEADS):
        x = _rms(qkv[:, hd * LANES:(hd + 1) * LANES], qn_ref[...])
        q_ref[:, hd * LANES:(hd + 1) * LANES] = (_rope(x, c, slo, shi, GQA_HEAD // 2) * scale).astype(BF16)
    for hd in range(GQA_KV_HEADS):
        x = _rms(qkv[:, n_q + hd * LANES:n_q + (hd + 1) * LANES], kn_ref[...])
        k_ref[:, hd * LANES:(hd + 1) * LANES] = _rope(x, c, slo, shi, GQA_HEAD // 2).astype(BF16)
    v_ref[...] = qkv[:, n_q + n_kv:].astype(BF16)


def gqa_layer(lay, h, mod, gain, w_qkv, q_norm, k_norm, w_o):
    d = lay.d
    tabs = rope_tables(lay, GQA_HEAD)
    n_q = GQA_HEADS * GQA_HEAD
    n_kv = GQA_KV_HEADS * GQA_HEAD
    q, k, v = pl.pallas_call(
        functools.partial(_gqa_qkv_body, float(GQA_HEAD) ** -0.5),
        grid=(lay.nblocks,),
        in_specs=[_rows(d), _full((1, d)), _modspec(lay), _full((d, n_q + 2 * n_kv)), _full((1, LANES)),
                  _full((1, LANES))] + [_posrows(lay)] * 3,
        out_specs=[_rows(n_q), _rows(n_kv), _rows(n_kv)],
        out_shape=[jax.ShapeDtypeStruct((lay.n, n_q), BF16), jax.ShapeDtypeStruct((lay.n, n_kv), BF16),
                   jax.ShapeDtypeStruct((lay.n, n_kv), BF16)],
        compiler_params=_cparams(("parallel",)),
        name="gqa_qkv",
    )(h, gain.reshape(1, d), mod, w_qkv.astype(BF16), q_norm.reshape(1, LANES), k_norm.reshape(1, LANES), *tabs)
    o = attention(lay, q, k, v, GQA_HEADS, GQA_HEADS // GQA_KV_HEADS, GQA_HEAD, GQA_HEAD)
    return out_proj(lay, o, w_o, jnp.zeros((d,), F32), h, mod)


def _row_tiles(d):
    return d // LANES


def _store_row_tiles(ref, x):
    rows, d = x.shape
    nt = _row_tiles(d)
    for j in range(nt):
        ref[pl.ds(j, rows, stride=nt), :] = x[:, j * LANES:(j + 1) * LANES]


def _load_row_tiles(ref, rows, d):
    nt = _row_tiles(d)
    return jnp.concatenate([ref[pl.ds(j, rows, stride=nt), :] for j in range(nt)], axis=-1)


def _ffn_prep_body(h_ref, g_ref, mod_ref, wr_ref, br_ref, tri_ref, v_ref, idx_ref, wt_ref, rank_ref, cnt_ref,
                   carry_ref):
    @pl.when(pl.program_id(0) == 0)
    def _():
        carry_ref[...] = jnp.zeros_like(carry_ref)

    v = _rms(h_ref[...], g_ref[...]) * (1.0 + mod_ref[0, 4:5, :]) + mod_ref[0, 3:4, :]
    _store_row_tiles(v_ref, v)
    logit = _dot(v, wr_ref[...]) + br_ref[...]
    lane = lax.broadcasted_iota(jnp.int32, logit.shape, 1).astype(F32)
    idx = jnp.zeros_like(logit)
    ex = jnp.zeros_like(logit)
    den = 0.0
    m0 = None
    hot = []
    for kx in range(TOP_K):
        m = jnp.max(logit, axis=-1, keepdims=True)
        am = jnp.min(jnp.where(logit == m, lane, float(LANES)), axis=-1, keepdims=True)
        if kx == 0:
            m0 = m
        e = jnp.exp(m - m0)
        den = den + e
        idx = jnp.where(lane == kx, am, idx)
        ex = jnp.where(lane == kx, e, ex)
        hot.append(lane == am)
        logit = jnp.where(hot[kx], -jnp.inf, logit)
    idx_ref[...] = idx.astype(jnp.int32)
    wt_ref[...] = ex / den
    chosen = jnp.where(hot[0] | hot[1] | hot[2] | hot[3], 1.0, 0.0)
    before = carry_ref[...] + jnp.dot(tri_ref[...], chosen.astype(BF16), preferred_element_type=F32)
    rank = jnp.zeros_like(logit)
    for kx in range(TOP_K):
        rk = jnp.sum(jnp.where(hot[kx], before, 0.0), axis=-1, keepdims=True)
        rank = jnp.where(lane == kx, rk, rank)
    rank_ref[...] = rank.astype(jnp.int32)
    carry_ref[...] = carry_ref[...] + jnp.sum(chosen, axis=0, keepdims=True)
    cnt_ref[...] = carry_ref[...]


def ffn_prep(lay, h, gain, mod, w_r, b_r):
    d = lay.d
    ne = w_r.shape[1]
    wr = jnp.pad(w_r, ((0, 0), (0, LANES - ne))).astype(BF16)
    br = jnp.concatenate([b_r, jnp.full((LANES - ne,), -1e30, F32)]).reshape(1, LANES)
    tri = jnp.asarray(np.tril(np.ones((TM, TM), np.float32), -1)).astype(BF16)
    return pl.pallas_call(
        _ffn_prep_body,
        grid=(lay.nblocks,),
        in_specs=[_rows(d), _full((1, d)), _modspec(lay), _full((d, LANES)), _full((1, LANES)), _full((TM, TM))],
        out_specs=[_rows(LANES, TM * _row_tiles(d)), _rows(LANES), _rows(LANES), _rows(LANES), _full((1, LANES))],
        out_shape=[jax.ShapeDtypeStruct((lay.n * _row_tiles(d), LANES), F32),
                   jax.ShapeDtypeStruct((lay.n, LANES), jnp.int32),
                   jax.ShapeDtypeStruct((lay.n, LANES), F32), jax.ShapeDtypeStruct((lay.n, LANES), jnp.int32),
                   jax.ShapeDtypeStruct((1, LANES), F32)],
        scratch_shapes=[pltpu.VMEM((1, LANES), F32)],
        compiler_params=_cparams(("arbitrary",)),
        name="ffn_prep",
    )(h, gain.reshape(1, d), mod, wr, br, tri)


def _row_at(ref, row, nt):
    return ref.at[pl.ds(pl.multiple_of(row * nt, nt), nt)]


def _dispatch_body(nt, dest_ref, v_ref, xs_in_ref, xs_ref, sem):
    del xs_in_ref

    def issue(t, carry):
        for kx in range(TOP_K):
            pltpu.make_async_copy(_row_at(v_ref, t, nt), _row_at(xs_ref, dest_ref[0, 0, t * TOP_K + kx], nt),
                                  sem).start()
        return carry

    lax.fori_loop(0, TM, issue, 0)

    def drain(t, carry):
        for kx in range(TOP_K):
            pltpu.make_async_copy(_row_at(v_ref, 0, nt), _row_at(xs_ref, 0, nt), sem).wait()
        return carry

    lax.fori_loop(0, TM, drain, 0)


def dispatch(lay, v, dest3, n_rows):
    nt = _row_tiles(lay.d)
    xs0 = jnp.zeros((n_rows * nt, LANES), v.dtype)
    return pl.pallas_call(
        functools.partial(_dispatch_body, nt),
        grid=(lay.nblocks,),
        in_specs=[pl.BlockSpec((1, 1, TM * TOP_K), lambda i: (i, 0, 0), memory_space=pltpu.SMEM),
                  _rows(LANES, TM * nt), pl.BlockSpec(memory_space=pl.ANY)],
        out_specs=pl.BlockSpec(memory_space=pl.ANY),
        out_shape=jax.ShapeDtypeStruct((n_rows * nt, LANES), v.dtype),
        scratch_shapes=[pltpu.SemaphoreType.DMA],
        input_output_aliases={2: 0},
        compiler_params=_cparams(("arbitrary",)),
        name="moe_dispatch",
    )(dest3, v, xs0)


def _experts_body(be_ref, bv_ref, x_ref, wg_ref, bg_ref, wu_ref, bu_ref, wd_ref, bd_ref, y_ref):
    i = pl.program_id(0)

    @pl.when(bv_ref[i] > 0)
    def _():
        x = _load_row_tiles(x_ref, EXPERT_ROWS, wg_ref.shape[1]).astype(BF16)
        gt = jnp.minimum(jnp.dot(x, wg_ref[0], preferred_element_type=F32) + bg_ref[0], SWIGLU_LIMIT)
        up = jnp.clip(jnp.dot(x, wu_ref[0], preferred_element_type=F32) + bu_ref[0], -SWIGLU_LIMIT, SWIGLU_LIMIT)
        act = gt * _sigmoid(SWIGLU_ALPHA * gt) * (up + 1.0)
        _store_row_tiles(y_ref, _dot(act, wd_ref[0]) + bd_ref[0])

    @pl.when(bv_ref[i] == 0)
    def _():
        y_ref[...] = jnp.zeros_like(y_ref)


def experts(layer, xs, blk_e, blk_valid, wg, bg, wu, bu, wd, bd):
    nl, ne, d, ff = wg.shape
    nt = _row_tiles(d)
    n_rows = xs.shape[0] // nt
    r = EXPERT_ROWS
    grid_spec = pltpu.PrefetchScalarGridSpec(
        num_scalar_prefetch=2,
        grid=(n_rows // r,),
        in_specs=[pl.BlockSpec((r * nt, LANES), lambda i, be, bv: (i, 0)),
                  pl.BlockSpec((None, 1, d, ff), lambda i, be, bv: (layer, be[i], 0, 0)),
                  pl.BlockSpec((None, 1, 1, ff), lambda i, be, bv: (layer, be[i], 0, 0)),
                  pl.BlockSpec((None, 1, d, ff), lambda i, be, bv: (layer, be[i], 0, 0)),
                  pl.BlockSpec((None, 1, 1, ff), lambda i, be, bv: (layer, be[i], 0, 0)),
                  pl.BlockSpec((None, 1, ff, d), lambda i, be, bv: (layer, be[i], 0, 0)),
                  pl.BlockSpec((None, 1, 1, d), lambda i, be, bv: (layer, be[i], 0, 0))],
        out_specs=pl.BlockSpec((r * nt, LANES), lambda i, be, bv: (i, 0)),
    )
    return pl.pallas_call(
        _experts_body,
        grid_spec=grid_spec,
        out_shape=jax.ShapeDtypeStruct((n_rows * nt, LANES), F32),
        compiler_params=_cparams(("arbitrary",)),
        name="experts",
    )(blk_e, blk_valid, xs, wg, bg.reshape(nl, ne, 1, ff), wu, bu.reshape(nl, ne, 1, ff), wd,
      bd.reshape(nl, ne, 1, d))


def _combine_body(nt, dest_ref, wt_ref, h_ref, mod_ref, y_ref, out_ref, ybuf, sem):
    def issue(t, carry):
        for kx in range(TOP_K):
            pltpu.make_async_copy(_row_at(y_ref, dest_ref[0, 0, t * TOP_K + kx], nt), _row_at(ybuf.at[kx], t, nt),
                                  sem).start()
        return carry

    lax.fori_loop(0, TM, issue, 0)

    def drain(t, carry):
        for kx in range(TOP_K):
            pltpu.make_async_copy(_row_at(y_ref, 0, nt), _row_at(ybuf.at[kx], 0, nt), sem).wait()
        return carry

    lax.fori_loop(0, TM, drain, 0)
    wt = wt_ref[...]
    d = h_ref.shape[1]
    f = wt[:, 0:1] * _load_row_tiles(ybuf.at[0], TM, d)
    for kx in range(1, TOP_K):
        f = f + wt[:, kx:kx + 1] * _load_row_tiles(ybuf.at[kx], TM, d)
    out_ref[...] = h_ref[...] + mod_ref[0, 5:6, :] * f


def combine(lay, h, y, dest3, wts, mod):
    d = lay.d
    nt = _row_tiles(d)
    return pl.pallas_call(
        functools.partial(_combine_body, nt),
        grid=(lay.nblocks,),
        in_specs=[pl.BlockSpec((1, 1, TM * TOP_K), lambda i: (i, 0, 0), memory_space=pltpu.SMEM),
                  _rows(LANES), _rows(d), _modspec(lay), pl.BlockSpec(memory_space=pl.ANY)],
        out_specs=_rows(d),
        out_shape=jax.ShapeDtypeStruct((lay.n, d), F32),
        scratch_shapes=[pltpu.VMEM((TOP_K, TM * nt, LANES), F32), pltpu.SemaphoreType.DMA],
        compiler_params=_cparams(("arbitrary",)),
        name="moe_combine",
    )(dest3, wts, h, mod, y)


def moe_layer(lay, layer, h, mod, gain, w_r, b_r, wg, bg, wu, bu, wd, bd):
    ne = w_r.shape[1]
    r = EXPERT_ROWS
    v, idx, wts, rank, cnt = ffn_prep(lay, h, gain, mod, w_r, b_r)
    n_slot = lay.n * TOP_K
    counts = cnt[0, :ne].astype(jnp.int32)
    padded = (counts + r - 1) // r * r
    pad_end = jnp.cumsum(padded)
    pad_start = pad_end - padded
    dest = pad_start[idx[:, :TOP_K]] + rank[:, :TOP_K]
    dest3 = dest.reshape(lay.nblocks, 1, TM * TOP_K)
    n_rows = n_slot + ne * r
    blk_start = jnp.arange(n_rows // r, dtype=jnp.int32) * r
    blk_e = jnp.minimum(jnp.searchsorted(pad_end, blk_start, side='right'), ne - 1).astype(jnp.int32)
    blk_valid = (blk_start < pad_end[-1]).astype(jnp.int32)
    xs = dispatch(lay, v, dest3, n_rows)
    y = experts(layer, xs, blk_e, blk_valid, wg, bg, wu, bu, wd, bd)
    return combine(lay, h, y, dest3, wts, mod)


def kernel(x, c, ctx, c_ctx, mod_w, mod_b, norm_mix, norm_ffn, router_w, router_b, exp_w_gate, exp_b_gate, exp_w_up, exp_b_up, exp_w_down, exp_b_down, rw_mu, rw_wr, rw_wk, rw_wv, rw_wo, rw_w0, rw_w1, rw_w2, rw_a0, rw_a1, rw_a2, rw_g1, rw_g2, rw_kk, rw_ka, rw_rk, rw_lnx_g, rw_lnx_b, mla_w_down, mla_qa_norm, mla_w_qb, mla_kv_norm, mla_w_kvb, mla_qn_nope, mla_qn_rope, mla_kn_nope, mla_kn_rope, mla_wo, fn_w, fn_b, gqa_w_qkv, gqa_q_norm, gqa_k_norm, gqa_wo):
    bsz, seq, d = x.shape
    ctx_len = ctx.shape[1]
    depth = mod_w.shape[0]
    lay = Layout(bsz, ctx_len, seq, d)
    assert bsz < 16 and d // PAIR * PAIR == d

    cin = jnp.zeros((16, d), F32).at[:bsz].set(c).at[bsz].set(c_ctx)
    mod_all = modulation(cin, mod_w, mod_b).reshape(depth, 16, 6, d)

    h = jnp.concatenate([ctx, x], axis=1).reshape(lay.n, d)
    wg_all, wu_all, wd_all = exp_w_gate.astype(BF16), exp_w_up.astype(BF16), exp_w_down.astype(BF16)
    n_mixers = 4
    for i in range(depth):
        m, j = i % n_mixers, i // n_mixers
        mod = mod_all[i]
        if m == 0:
            h = rwkv_layer(lay, h, mod, norm_mix[i], rw_mu[j], rw_wr[j], rw_wk[j], rw_wv[j], rw_wo[j], rw_w0[j],
                           rw_w1[j], rw_w2[j], rw_a0[j], rw_a1[j], rw_a2[j], rw_g1[j], rw_g2[j], rw_kk[j],
                           rw_ka[j], rw_rk[j], rw_lnx_g[j], rw_lnx_b[j])
        elif m == 1:
            h = mla_layer(lay, h, mod, norm_mix[i], mla_w_down[j], mla_qa_norm[j], mla_w_qb[j], mla_kv_norm[j],
                          mla_w_kvb[j], mla_qn_nope[j], mla_qn_rope[j], mla_kn_nope[j], mla_kn_rope[j], mla_wo[j])
        elif m == 2:
            h = fourier_layer(lay, h, mod, norm_mix[i], fn_w[j], fn_b[j])
        else:
            h = gqa_layer(lay, h, mod, norm_mix[i], gqa_w_qkv[j], gqa_q_norm[j], gqa_k_norm[j], gqa_wo[j])
        h = moe_layer(lay, i, h, mod, norm_ffn[i], router_w[i], router_b[i], wg_all, exp_b_gate, wu_all, exp_b_up,
                      wd_all, exp_b_down)
    return h.reshape(bsz, lay.tt, d)[:, ctx_len:, :]
```

```python
import functools
import math

import numpy as np
import jax
import jax.numpy as jnp
from jax import lax
from jax.experimental import pallas as pl
from jax.experimental.pallas import tpu as pltpu

F32 = jnp.float32
BF16 = jnp.bfloat16

LANES = 128
SUBLANES = 8
VMEM_LIMIT_BYTES = 56 * 1024 * 1024

GRID_W = 64
ROPE_THETA = 10000.0
NORM_EPS = 1e-6
RW_HEAD = 64
RW_GN_EPS = 64e-5
MLA_HEADS = 16
MLA_Q_RANK = 512
MLA_KV_RANK = 512
MLA_NOPE = 128
MLA_ROPE = 64
MLA_V = 128
FN_GROUPS = 8
GQA_HEADS = 16
GQA_KV_HEADS = 4
GQA_HEAD = 128
TOP_K = 4
SWIGLU_LIMIT = 7.0
SWIGLU_ALPHA = 1.702

TM = 256
WKV_CHUNK = 64
WKV_GROUP = 16
ATTN_ROWS = 256
EXPERT_ROWS = 256
PAIR = 2 * RW_HEAD


def _cparams(sem):
    return pltpu.CompilerParams(dimension_semantics=sem, vmem_limit_bytes=VMEM_LIMIT_BYTES)


def _dot(a, b):
    return jnp.dot(a.astype(BF16), b.astype(BF16), preferred_element_type=F32)


def _dot_nt(a, b):
    return lax.dot_general(a.astype(BF16), b.astype(BF16), (((1,), (1,)), ((), ())),
                           preferred_element_type=F32)


def _rms(x, gain):
    return x * lax.rsqrt(jnp.mean(x * x, axis=-1, keepdims=True) + NORM_EPS) * gain


def _sigmoid(x):
    return 1.0 / (1.0 + jnp.exp(-x))


class Layout:
    def __init__(self, batch, ctx_len, seq, d):
        self.b, self.ctx, self.seq, self.d = batch, ctx_len, seq, d
        self.tt = ctx_len + seq
        self.n = batch * self.tt
        assert ctx_len % TM == 0 and seq % TM == 0
        self.nb = self.tt // TM
        self.cb = ctx_len // TM
        self.nblocks = self.n // TM

    def seg(self, i):
        return jnp.where(i % self.nb < self.cb, self.b, i // self.nb)


def _rows(ncols, tm=TM):
    return pl.BlockSpec((tm, ncols), lambda i: (i, 0))


def _full(shape):
    nd = len(shape)
    return pl.BlockSpec(shape, lambda i: (0,) * nd)


def _modspec(lay):
    return pl.BlockSpec((1, 6, lay.d), lambda i: (lay.seg(i), 0, 0))


def _posrows(lay, ncols=LANES):
    return pl.BlockSpec((TM, ncols), lambda i: (i % lay.nb, 0))


def _pairs(tm=TM):
    return pl.BlockSpec((None, tm, PAIR), lambda i: (0, i, 0))


def _mod_body(c_ref, w_ref, b_ref, o_ref):
    c = c_ref[...]
    s = c * _sigmoid(c)
    o_ref[0] = _dot(s, w_ref[0]) + b_ref[0]


def modulation(cin, mod_w, mod_b):
    depth, d, n6 = mod_w.shape
    tn = 1024
    return pl.pallas_call(
        _mod_body,
        grid=(depth, n6 // tn),
        in_specs=[pl.BlockSpec((16, d), lambda l, j: (0, 0)),
                  pl.BlockSpec((1, d, tn), lambda l, j: (l, 0, j)),
                  pl.BlockSpec((1, 1, tn), lambda l, j: (l, 0, j))],
        out_specs=pl.BlockSpec((1, 16, tn), lambda l, j: (l, 0, j)),
        out_shape=jax.ShapeDtypeStruct((depth, 16, n6), F32),
        compiler_params=_cparams(("parallel", "parallel")),
        name="modulation",
    )(cin, mod_w, mod_b.reshape(depth, 1, n6))


def _rw_prep_body(lay, h_ref, hp_ref, hn_ref, g_ref, mod_ref, mu_ref, *outs):
    j = pl.program_id(0) % lay.nb
    seg_start = jnp.logical_or(j == 0, j == lay.cb)
    seg_end = jnp.logical_or(j == lay.cb - 1, j == lay.nb - 1)
    gain = g_ref[...]
    shift, scale = mod_ref[0, 0:1, :], mod_ref[0, 1:2, :]

    def norm_mod(x):
        return _rms(x, gain) * (1.0 + scale) + shift

    u = norm_mod(h_ref[...])
    prev = jnp.where(seg_start, 0.0, norm_mod(hp_ref[SUBLANES - 1:SUBLANES, :]))
    nxt = jnp.where(seg_end, 0.0, norm_mod(hn_ref[0:1, :]))
    row = lax.broadcasted_iota(jnp.int32, u.shape, 0)
    up = jnp.where(row == 0, prev, pltpu.roll(u, 1, 0))
    un = jnp.where(row == TM - 1, nxt, pltpu.roll(u, TM - 1, 0))
    du = 0.5 * (up + un) - u
    for n, o_ref in enumerate(outs):
        o_ref[...] = (u + du * mu_ref[n:n + 1, :]).astype(BF16)


def rw_prep(lay, h, gain, mod, mu):
    d = lay.d
    r8 = TM // SUBLANES
    last8 = lay.n // SUBLANES - 1
    mu8 = jnp.concatenate([mu, jnp.zeros((2, d), F32)], axis=0)
    return pl.pallas_call(
        functools.partial(_rw_prep_body, lay),
        grid=(lay.nblocks,),
        in_specs=[_rows(d),
                  pl.BlockSpec((SUBLANES, d), lambda i: (jnp.maximum(i * r8 - 1, 0), 0)),
                  pl.BlockSpec((SUBLANES, d), lambda i: (jnp.minimum((i + 1) * r8, last8), 0)),
                  _full((1, d)), _modspec(lay), _full((8, d))],
        out_specs=[_rows(d)] * 6,
        out_shape=[jax.ShapeDtypeStruct((lay.n, d), BF16)] * 6,
        compiler_params=_cparams(("parallel",)),
        name="rw_prep",
    )(h, h, h, gain.reshape(1, d), mod, mu8)


def _linear_pairs_body(npairs, x_ref, w_ref, o_ref):
    y = _dot(x_ref[...], w_ref[...])
    for p in range(npairs):
        o_ref[p] = y[:, p * PAIR:(p + 1) * PAIR]


def linear_pairs(lay, x, w):
    d = lay.d
    npairs = d // PAIR
    return pl.pallas_call(
        functools.partial(_linear_pairs_body, npairs),
        grid=(lay.nblocks,),
        in_specs=[_rows(d), _full((d, d))],
        out_specs=pl.BlockSpec((npairs, TM, PAIR), lambda i: (0, i, 0)),
        out_shape=jax.ShapeDtypeStruct((npairs, lay.n, PAIR), F32),
        compiler_params=_cparams(("parallel",)),
        name="rw_linear",
    )(x, w)


def _rw_lora_body(npairs, xw_ref, xa_ref, xg_ref, w1_ref, w2_ref, a1_ref, a2_ref, g1_ref, g2_ref,
                  w0_ref, a0_ref, lw_ref, ag_ref, g_ref):
    hw = jnp.tanh(_dot(xw_ref[...], w1_ref[...]))
    ha = _dot(xa_ref[...], a1_ref[...])
    hg = _sigmoid(_dot(xg_ref[...], g1_ref[...]))
    rw = w1_ref.shape[1] // 2
    rg = g1_ref.shape[1] // 2
    for dr in range(2):
        z = w0_ref[dr:dr + 1, :] + _dot(hw[:, dr * rw:(dr + 1) * rw], w2_ref[dr])
        lw = -math.exp(-0.5) * _sigmoid(z)
        a = _sigmoid(a0_ref[dr:dr + 1, :] + _dot(ha[:, dr * rw:(dr + 1) * rw], a2_ref[dr]))
        g = _dot(hg[:, dr * rg:(dr + 1) * rg], g2_ref[dr])
        for p in range(npairs):
            sl = slice(p * PAIR, (p + 1) * PAIR)
            lw_ref[dr, p] = lw[:, sl]
            ag_ref[dr, p] = a[:, sl]
            g_ref[dr, p] = g[:, sl]


def _pad_lora(w1, w2):
    r = w1.shape[2]
    rp = -(-r // LANES) * LANES
    w1p = jnp.pad(w1, ((0, 0), (0, 0), (0, rp - r)))
    w1p = jnp.concatenate([w1p[0], w1p[1]], axis=1).astype(BF16)
    w2p = jnp.pad(w2, ((0, 0), (0, rp - r), (0, 0))).astype(BF16)
    return w1p, w2p


def rw_lora(lay, xw, xa, xg, w0, w1, w2, a0, a1, a2, g1, g2):
    d = lay.d
    npairs = d // PAIR
    w1p, w2p = _pad_lora(w1, w2)
    a1p, a2p = _pad_lora(a1, a2)
    g1p, g2p = _pad_lora(g1, g2)
    ospec = pl.BlockSpec((2, npairs, TM, PAIR), lambda i: (0, 0, i, 0))
    oshape = jax.ShapeDtypeStruct((2, npairs, lay.n, PAIR), F32)
    return pl.pallas_call(
        functools.partial(_rw_lora_body, npairs),
        grid=(lay.nblocks,),
        in_specs=[_rows(d)] * 3 + [_full(w1p.shape), _full(w2p.shape), _full(a1p.shape), _full(a2p.shape),
                                    _full(g1p.shape), _full(g2p.shape), _full((2, d)), _full((2, d))],
        out_specs=[ospec] * 3,
        out_shape=[oshape] * 3,
        compiler_params=_cparams(("parallel",)),
        name="rw_lora",
    )(xw, xa, xg, w1p, w2p, a1p, a2p, g1p, g2p, w0, a0)


def _bd(z, head0):
    return jnp.concatenate([jnp.where(head0, z, 0.0), jnp.where(head0, 0.0, z)], axis=0)


def _head_sum(x, head0):
    s0 = jnp.sum(jnp.where(head0, x, 0.0), axis=-1, keepdims=True)
    s1 = jnp.sum(jnp.where(head0, 0.0, x), axis=-1, keepdims=True)
    return jnp.where(head0, s0, s1)


def _wkv_group_chunk(rev, tiles, par, hts):
    c = WKV_CHUNK
    rng = range(len(tiles))
    t = lax.broadcasted_iota(jnp.int32, (c, PAIR), 0)
    lane = lax.broadcasted_iota(jnp.int32, (c, PAIR), 1)
    s = lane % RW_HEAD
    head0 = lane < RW_HEAD
    strict = (s > t) if rev else (s < t)
    incl = (s >= t) if rev else (s <= t)
    eye = jnp.where(s == t, 1.0, 0.0)
    rr = lax.broadcasted_iota(jnp.int32, (PAIR, PAIR), 0) // RW_HEAD
    cc = lax.broadcasted_iota(jnp.int32, (PAIR, PAIR), 1) // RW_HEAD
    same_head = rr == cc

    def prep(q):
        r, k, v, lw, ag, _ = tiles[q]
        kk_p, ka_p = par[q][0], par[q][1]
        kkf = k * kk_p
        kk = kkf / jnp.maximum(jnp.sqrt(_head_sum(kkf * kkf, head0)), 1e-12)
        kd = k * (1.0 + (ag - 1.0) * ka_p)
        bvec = kk * ag
        cs = lw
        for sh in (1, 2, 4, 8, 16, 32):
            if rev:
                cs = cs + jnp.where(t + sh < c, pltpu.roll(cs, c - sh, 0), 0.0)
            else:
                cs = cs + jnp.where(t >= sh, pltpu.roll(cs, sh, 0), 0.0)
        cl = cs[0:1, :] if rev else cs[c - 1:c, :]
        e_neg = jnp.exp(-cs)
        e_end = jnp.exp(cl - cs)
        ar = jnp.concatenate([-kk * jnp.exp(cs - lw), r * jnp.exp(cs)], axis=0).astype(BF16)
        bk = jnp.concatenate([_bd(bvec * e_neg, head0), _bd(kd * e_neg, head0)], axis=0).astype(BF16)
        bk_end = jnp.concatenate([bvec * e_end, kd * e_end], axis=0).astype(BF16)
        return dict(ar=ar, bk=bk, bk_end=bk_end, kd=kd, decay=jnp.exp(cl), vbd=_bd(v, head0).astype(BF16))

    st = [prep(q) for q in rng]
    p = [_dot_nt(st[q]["ar"], st[q]["bk"]) for q in rng]
    arh = [_dot_nt(st[q]["ar"], hts[q]) for q in rng]
    l_ab = [jnp.where(strict, p[q][0:c, 0:PAIR], 0.0) for q in rng]
    l_ak = [jnp.where(strict, p[q][0:c, PAIR:2 * PAIR], 0.0) for q in rng]
    m_r = [jnp.concatenate([jnp.where(incl, p[q][c:2 * c, 0:PAIR], 0.0),
                            jnp.where(incl, p[q][c:2 * c, PAIR:2 * PAIR], 0.0)], axis=1).astype(BF16) for q in rng]
    x = [arh[q][0:c] + _dot(l_ak[q], st[q]["vbd"]) for q in rng]
    tinv = [eye + l_ab[q] for q in rng]
    pw = l_ab
    for _ in range(5):
        pw = [_dot(pw[q], _bd(pw[q], head0)) for q in rng]
        tinv = [tinv[q] + _dot(tinv[q], _bd(pw[q], head0)) for q in rng]
    u = [_dot(tinv[q], _bd(x[q], head0)) for q in rng]
    y = [arh[q][c:2 * c] + _dot(m_r[q], jnp.concatenate([_bd(u[q], head0).astype(BF16), st[q]["vbd"]], axis=0))
         for q in rng]
    upd = [_dot(jnp.concatenate([u[q], tiles[q][2]], axis=0).T, st[q]["bk_end"]) for q in rng]
    ht_new = [hts[q] * st[q]["decay"] + jnp.where(same_head, upd[q], 0.0) for q in rng]

    outs = []
    inv_n = 1.0 / RW_HEAD
    for q in rng:
        r, _, v, _, _, g = tiles[q]
        rk_p, lg_p, lb_p = par[q][2], par[q][3], par[q][4]
        mu = _head_sum(y[q], head0) * inv_n
        yc = y[q] - mu
        var = _head_sum(yc * yc, head0) * inv_n
        bonus = _head_sum(r * st[q]["kd"] * rk_p, head0) * v
        outs.append((yc * lax.rsqrt(var + RW_GN_EPS) * lg_p + lb_p + bonus) * g)
    return outs, ht_new


def _wkv_body(rev, npairs, r_ref, k_ref, v_ref, lw_ref, ag_ref, g_ref, kk_ref, ka_ref, rk_ref, lg_ref, lb_ref,
              o_ref, ht_ref):
    @pl.when(pl.program_id(1) == 0)
    def _():
        ht_ref[...] = jnp.zeros_like(ht_ref)

    def group(gi, carry):
        ps = [gi * WKV_GROUP + q for q in range(WKV_GROUP)]
        tiles = [(r_ref[p], k_ref[p], v_ref[p], lw_ref[0, p], ag_ref[0, p], g_ref[0, p]) for p in ps]
        par = [(kk_ref[p], ka_ref[p], rk_ref[p], lg_ref[p], lb_ref[p]) for p in ps]
        outs, hts = _wkv_group_chunk(rev, tiles, par, [ht_ref[p] for p in ps])
        for q, p in enumerate(ps):
            o_ref[p] = outs[q]
            ht_ref[p] = hts[q]
        return carry

    lax.fori_loop(0, npairs // WKV_GROUP, group, 0)


def wkv(lay, rev, r, k, v, lw, ag, g, params):
    npairs = lay.d // PAIR
    c = WKV_CHUNK
    nch = lay.tt // c
    cch = lay.ctx // c
    dr = 1 if rev else 0

    def rowblk(b, ci):
        if rev:
            return b * nch + jnp.where(ci < cch, cch - 1 - ci, nch + cch - 1 - ci)
        return b * nch + ci

    tok = pl.BlockSpec((npairs, c, PAIR), lambda b, ci: (0, rowblk(b, ci), 0))
    tokd = pl.BlockSpec((1, npairs, c, PAIR), lambda b, ci: (dr, 0, rowblk(b, ci), 0))
    par = pl.BlockSpec((npairs, 1, PAIR), lambda b, ci: (0, 0, 0))
    return pl.pallas_call(
        functools.partial(_wkv_body, rev, npairs),
        grid=(lay.b, nch),
        in_specs=[tok, tok, tok, tokd, tokd, tokd] + [par] * 5,
        out_specs=tok,
        out_shape=jax.ShapeDtypeStruct((npairs, lay.n, PAIR), F32),
        scratch_shapes=[pltpu.VMEM((npairs, PAIR, PAIR), F32)],
        compiler_params=_cparams(("parallel", "arbitrary")),
        name="wkv_rev" if rev else "wkv_fwd",
    )(r, k, v, lw, ag, g, *params)


def _rw_out_body(npairs, o0_ref, o1_ref, h_ref, mod_ref, w_ref, out_ref):
    acc = jnp.concatenate([o0_ref[p] + o1_ref[p] for p in range(npairs)], axis=-1)
    out_ref[...] = h_ref[...] + mod_ref[0, 2:3, :] * _dot(acc, w_ref[...])


def rw_out(lay, o0, o1, h, mod, w):
    d = lay.d
    npairs = d // PAIR
    pm = pl.BlockSpec((npairs, TM, PAIR), lambda i: (0, i, 0))
    return pl.pallas_call(
        functools.partial(_rw_out_body, npairs),
        grid=(lay.nblocks,),
        in_specs=[pm, pm, _rows(d), _modspec(lay), _full((d, d))],
        out_specs=_rows(d),
        out_shape=jax.ShapeDtypeStruct((lay.n, d), F32),
        compiler_params=_cparams(("parallel",)),
        name="rw_out",
    )(o0, o1, h, mod, w)


def rwkv_layer(lay, h, mod, gain, mu, w_r, w_k, w_v, w_o, w0, w1, w2, a0, a1, a2, g1, g2, k_k, k_a, r_k,
               lnx_g, lnx_b):
    npairs = lay.d // PAIR
    xr, xw, xk, xv, xa, xg = rw_prep(lay, h, gain, mod, mu)
    r = linear_pairs(lay, xr, w_r.astype(BF16))
    k = linear_pairs(lay, xk, w_k.astype(BF16))
    v = linear_pairs(lay, xv, w_v.astype(BF16))
    lw, ag, g = rw_lora(lay, xw, xa, xg, w0, w1, w2, a0, a1, a2, g1, g2)
    params = [t.reshape(npairs, 1, PAIR) for t in (k_k, k_a, r_k.reshape(-1), lnx_g, lnx_b)]
    o0 = wkv(lay, False, r, k, v, lw, ag, g, params)
    o1 = wkv(lay, True, r, k, v, lw, ag, g, params)
    return rw_out(lay, o0, o1, h, mod, w_o.astype(BF16))


def rope_tables(lay, rot_dim):
    quarter = rot_dim // 4
    half = rot_dim // 2
    t = np.arange(lay.seq)
    inv = ROPE_THETA ** (-np.arange(quarter, dtype=np.float32) / quarter)
    ang = np.concatenate([(t // GRID_W)[:, None].astype(np.float32) * inv,
                          (t % GRID_W)[:, None].astype(np.float32) * inv], axis=-1)
    ang = jnp.asarray(ang, F32)
    cos, sin = jnp.cos(ang), jnp.sin(ang)
    pad = LANES - rot_dim
    zer = jnp.zeros((lay.seq, half), F32)
    c = jnp.concatenate([cos, cos, jnp.ones((lay.seq, pad), F32)], axis=-1)
    s_lo = jnp.concatenate([-sin, zer, jnp.zeros((lay.seq, pad), F32)], axis=-1)
    s_hi = jnp.concatenate([zer, sin, jnp.zeros((lay.seq, pad), F32)], axis=-1)

    def with_ctx(tab, fill):
        return jnp.concatenate([jnp.full((lay.ctx, LANES), fill, F32), tab], axis=0)

    return with_ctx(c, 1.0), with_ctx(s_lo, 0.0), with_ctx(s_hi, 0.0)


def _rope(x, c, s_lo, s_hi, half):
    return x * c + pltpu.roll(x, LANES - half, 1) * s_lo + pltpu.roll(x, half, 1) * s_hi


def _attn_rows(ctx, rc, row0, q_ref, k, v, o_ref):
    n = q_ref.shape[0] // rc
    ss = []
    for c in range(n):
        s = _dot_nt(q_ref[c * rc:(c + 1) * rc, :], k)
        if row0 is not None and row0 + c * rc < ctx:
            qrow = row0 + c * rc + lax.broadcasted_iota(jnp.int32, s.shape, 0)
            kcol = lax.broadcasted_iota(jnp.int32, s.shape, 1)
            s = jnp.where(jnp.logical_or(qrow >= ctx, kcol < ctx), s, -1e30)
        ss.append(s)
    ms = [jnp.max(s, axis=-1, keepdims=True) for s in ss]
    ps = [jnp.exp(ss[c] - ms[c]) for c in range(n)]
    ls = [jnp.sum(p, axis=-1, keepdims=True) for p in ps]
    for c in range(n):
        o_ref[c * rc:(c + 1) * rc, :] = (_dot(ps[c], v) / ls[c]).astype(o_ref.dtype)


def _attn_body(ctx, q_ref, k_ref, v_ref, o_ref):
    k = k_ref[...]
    v = v_ref[...]

    @pl.when(pl.program_id(2) == 0)
    def _():
        _attn_rows(ctx, ATTN_ROWS, 0, q_ref, k, v, o_ref)

    @pl.when(pl.program_id(2) != 0)
    def _():
        _attn_rows(ctx, ATTN_ROWS, None, q_ref, k, v, o_ref)


def attention(lay, q, k, v, n_heads, kv_group, dq, dv, tq=768):
    nq = lay.tt // tq
    assert lay.ctx <= tq and tq % ATTN_ROWS == 0
    k3 = k.reshape(lay.b, lay.tt, k.shape[1])
    v3 = v.reshape(lay.b, lay.tt, v.shape[1])
    return pl.pallas_call(
        functools.partial(_attn_body, lay.ctx),
        grid=(lay.b, n_heads, nq),
        in_specs=[pl.BlockSpec((tq, dq), lambda b, h, i: (b * nq + i, h)),
                  pl.BlockSpec((None, lay.tt, dq), lambda b, h, i: (b, 0, h // kv_group)),
                  pl.BlockSpec((None, lay.tt, dv), lambda b, h, i: (b, 0, h // kv_group))],
        out_specs=pl.BlockSpec((tq, dv), lambda b, h, i: (b * nq + i, h)),
        out_shape=jax.ShapeDtypeStruct((lay.n, n_heads * dv), BF16),
        compiler_params=_cparams(("parallel", "parallel", "arbitrary")),
        name="attention",
    )(q, k3, v3)


def _out_proj_body(x_ref, w_ref, b_ref, h_ref, mod_ref, out_ref):
    y = _dot(x_ref[...], w_ref[...]) + b_ref[...]
    out_ref[...] = h_ref[...] + mod_ref[0, 2:3, :] * y


def out_proj(lay, x, w, bias, h, mod):
    d = lay.d
    kdim = x.shape[1]
    return pl.pallas_call(
        _out_proj_body,
        grid=(lay.nblocks,),
        in_specs=[_rows(kdim), _full((kdim, d)), _full((1, d)), _rows(d), _modspec(lay)],
        out_specs=_rows(d),
        out_shape=jax.ShapeDtypeStruct((lay.n, d), F32),
        compiler_params=_cparams(("parallel",)),
        name="out_proj",
    )(x, w.astype(BF16), bias.reshape(1, d), h, mod)


def _mla_down_body(h_ref, g_ref, mod_ref, w_ref, qan_ref, kvn_ref, krn_ref, c_ref, slo_ref, shi_ref,
                   q_ref, ckv_ref, kr_ref):
    u = _rms(h_ref[...], g_ref[...]) * (1.0 + mod_ref[0, 1:2, :]) + mod_ref[0, 0:1, :]
    down = _dot(u, w_ref[...])
    q_ref[...] = _rms(down[:, 0:MLA_Q_RANK], qan_ref[...]).astype(BF16)
    ckv_ref[...] = _rms(down[:, MLA_Q_RANK:MLA_Q_RANK + MLA_KV_RANK], kvn_ref[...]).astype(BF16)
    kr = down[:, MLA_Q_RANK + MLA_KV_RANK:]
    ms = jnp.sum(kr * kr, axis=-1, keepdims=True) * (1.0 / MLA_ROPE)
    kr = kr * lax.rsqrt(ms + NORM_EPS) * krn_ref[...]
    kr_ref[...] = _rope(kr, c_ref[...], slo_ref[...], shi_ref[...], MLA_ROPE // 2).astype(BF16)


def _mla_q_body(scale, x_ref, w_ref, nn_ref, rn_ref, c_ref, slo_ref, shi_ref, q_ref):
    q = _dot(x_ref[...], w_ref[...])
    hw = 2 * LANES
    for hd in range(MLA_HEADS):
        nope = q[:, hd * hw:hd * hw + LANES]
        rope = q[:, hd * hw + LANES:(hd + 1) * hw]
        nope = _rms(nope, nn_ref[...])
        ms = jnp.sum(rope * rope, axis=-1, keepdims=True) * (1.0 / MLA_ROPE)
        rope = rope * lax.rsqrt(ms + NORM_EPS) * rn_ref[...]
        rope = _rope(rope, c_ref[...], slo_ref[...], shi_ref[...], MLA_ROPE // 2)
        q_ref[:, hd * hw:hd * hw + LANES] = (nope * scale).astype(BF16)
        q_ref[:, hd * hw + LANES:(hd + 1) * hw] = (rope * scale).astype(BF16)


def _mla_kv_body(x_ref, w_ref, kn_ref, kr_ref, k_ref, v_ref):
    kvb = _dot(x_ref[...], w_ref[...])
    hw = 2 * LANES
    kr = kr_ref[...]
    for hd in range(MLA_HEADS):
        k_ref[:, hd * hw:hd * hw + LANES] = _rms(kvb[:, hd * LANES:(hd + 1) * LANES], kn_ref[...]).astype(BF16)
        k_ref[:, hd * hw + LANES:(hd + 1) * hw] = kr
    v_ref[...] = kvb[:, MLA_HEADS * LANES:].astype(BF16)


def mla_layer(lay, h, mod, gain, w_down, qa_norm, w_qb, kv_norm, w_kvb, qn_nope, qn_rope, kn_nope, kn_rope, w_o):
    d = lay.d
    tabs = rope_tables(lay, MLA_ROPE)
    pad64 = lambda t: jnp.concatenate([t, jnp.zeros((LANES - MLA_ROPE,), F32)]).reshape(1, LANES)
    wd = jnp.pad(w_down, ((0, 0), (0, LANES - MLA_ROPE))).astype(BF16)
    nd = wd.shape[1]
    q_lat, ckv, kr = pl.pallas_call(
        _mla_down_body,
        grid=(lay.nblocks,),
        in_specs=[_rows(d), _full((1, d)), _modspec(lay), _full((d, nd)), _full((1, MLA_Q_RANK)),
                  _full((1, MLA_KV_RANK)), _full((1, LANES))] + [_posrows(lay)] * 3,
        out_specs=[_rows(MLA_Q_RANK), _rows(MLA_KV_RANK), _rows(LANES)],
        out_shape=[jax.ShapeDtypeStruct((lay.n, MLA_Q_RANK), BF16), jax.ShapeDtypeStruct((lay.n, MLA_KV_RANK), BF16),
                   jax.ShapeDtypeStruct((lay.n, LANES), BF16)],
        compiler_params=_cparams(("parallel",)),
        name="mla_down",
    )(h, gain.reshape(1, d), mod, wd, qa_norm.reshape(1, -1), kv_norm.reshape(1, -1), pad64(kn_rope), *tabs)

    hq = MLA_NOPE + MLA_ROPE
    wq = w_qb.reshape(MLA_Q_RANK, MLA_HEADS, hq)
    wq = jnp.pad(wq, ((0, 0), (0, 0), (0, 2 * LANES - hq))).reshape(MLA_Q_RANK, MLA_HEADS * 2 * LANES).astype(BF16)
    dq = 2 * LANES
    scale = float(hq) ** -0.5
    q = pl.pallas_call(
        functools.partial(_mla_q_body, scale),
        grid=(lay.nblocks,),
        in_specs=[_rows(MLA_Q_RANK), _full(wq.shape), _full((1, LANES)), _full((1, LANES))] + [_posrows(lay)] * 3,
        out_specs=_rows(MLA_HEADS * dq),
        out_shape=jax.ShapeDtypeStruct((lay.n, MLA_HEADS * dq), BF16),
        compiler_params=_cparams(("parallel",)),
        name="mla_q",
    )(q_lat, wq, qn_nope.reshape(1, LANES), pad64(qn_rope), *tabs)

    wkv_ = w_kvb.reshape(MLA_KV_RANK, MLA_HEADS, MLA_NOPE + MLA_V)
    wkv_ = jnp.concatenate([wkv_[:, :, :MLA_NOPE].reshape(MLA_KV_RANK, -1),
                            wkv_[:, :, MLA_NOPE:].reshape(MLA_KV_RANK, -1)], axis=1).astype(BF16)
    k, v = pl.pallas_call(
        _mla_kv_body,
        grid=(lay.nblocks,),
        in_specs=[_rows(MLA_KV_RANK), _full(wkv_.shape), _full((1, LANES)), _rows(LANES)],
        out_specs=[_rows(MLA_HEADS * dq), _rows(MLA_HEADS * MLA_V)],
        out_shape=[jax.ShapeDtypeStruct((lay.n, MLA_HEADS * dq), BF16),
                   jax.ShapeDtypeStruct((lay.n, MLA_HEADS * MLA_V), BF16)],
        compiler_params=_cparams(("parallel",)),
        name="mla_kv",
    )(ckv, wkv_, kn_nope.reshape(1, LANES), kr)

    o = attention(lay, q, k, v, MLA_HEADS, 1, dq, MLA_V)
    return out_proj(lay, o, w_o, jnp.zeros((d,), F32), h, mod)


def _dft_cos_sin(n):
    j = np.arange(n, dtype=np.int64)
    m = (j[:, None] * j[None, :]) % n
    ang = 2.0 * np.pi * m.astype(np.float64) / n
    return np.cos(ang), np.sin(ang)


def _fn_stage1_body(gw, h_ref, g_ref, mod_ref, cs_ref, zc_ref, zs_ref):
    u = (_rms(h_ref[...], g_ref[...]) * (1.0 + mod_ref[0, 1:2, :]) + mod_ref[0, 0:1, :]).astype(BF16)
    for gi in range(FN_GROUPS):
        z = jnp.dot(u[:, gi * gw:(gi + 1) * gw], cs_ref[...], preferred_element_type=F32)
        zc_ref[:, gi * gw:(gi + 1) * gw] = z[:, 0:gw].astype(BF16)
        zs_ref[:, gi * gw:(gi + 1) * gw] = z[:, gw:2 * gw].astype(BF16)


def _fn_stage2_body(lay, ct_ref, st_ref, zc_ref, zs_ref, o_ref):
    j = pl.program_id(1)

    def mix(lo, hi):
        f = (jnp.dot(ct_ref[:, lo:hi], zc_ref[lo:hi, :], preferred_element_type=F32)
             - jnp.dot(st_ref[:, lo:hi], zs_ref[lo:hi, :], preferred_element_type=F32))
        o_ref[...] = f.astype(BF16)

    @pl.when(j < lay.cb)
    def _():
        mix(0, lay.ctx)

    @pl.when(j >= lay.cb)
    def _():
        mix(lay.ctx, lay.tt)


def fourier_layer(lay, h, mod, gain, w, b):
    d = lay.d
    gw = d // FN_GROUPS
    cw, sw = _dft_cos_sin(gw)
    csw = jnp.asarray(np.concatenate([cw, sw], axis=1), F32).astype(BF16)
    zc, zs = pl.pallas_call(
        functools.partial(_fn_stage1_body, gw),
        grid=(lay.nblocks,),
        in_specs=[_rows(d), _full((1, d)), _modspec(lay), _full((gw, 2 * gw))],
        out_specs=[_rows(d), _rows(d)],
        out_shape=[jax.ShapeDtypeStruct((lay.n, d), BF16)] * 2,
        compiler_params=_cparams(("parallel",)),
        name="fn_stage1",
    )(h, gain.reshape(1, d), mod, csw)

    ct = np.zeros((lay.tt, lay.tt), np.float64)
    st = np.zeros((lay.tt, lay.tt), np.float64)
    for lo, n in ((0, lay.ctx), (lay.ctx, lay.seq)):
        c, s = _dft_cos_sin(n)
        sc = 1.0 / math.sqrt(n * gw)
        ct[lo:lo + n, lo:lo + n] = c * sc
        st[lo:lo + n, lo:lo + n] = s * sc
    ct = jnp.asarray(ct, F32).astype(BF16)
    st = jnp.asarray(st, F32).astype(BF16)
    zc3 = zc.reshape(lay.b, lay.tt, d)
    zs3 = zs.reshape(lay.b, lay.tt, d)
    f = pl.pallas_call(
        functools.partial(_fn_stage2_body, lay),
        grid=(lay.b, lay.nb),
        in_specs=[pl.BlockSpec((TM, lay.tt), lambda bb, j: (j, 0)),
                  pl.BlockSpec((TM, lay.tt), lambda bb, j: (j, 0)),
                  pl.BlockSpec((None, lay.tt, d), lambda bb, j: (bb, 0, 0)),
                  pl.BlockSpec((None, lay.tt, d), lambda bb, j: (bb, 0, 0))],
        out_specs=pl.BlockSpec((TM, d), lambda bb, j: (bb * lay.nb + j, 0)),
        out_shape=jax.ShapeDtypeStruct((lay.n, d), BF16),
        compiler_params=_cparams(("parallel", "arbitrary")),
        name="fn_stage2",
    )(ct, st, zc3, zs3)
    return out_proj(lay, f, w, b, h, mod)


def _gqa_qkv_body(scale, h_ref, g_ref, mod_ref, w_ref, qn_ref, kn_ref, c_ref, slo_ref, shi_ref,
                  q_ref, k_ref, v_ref):
    u = _rms(h_ref[...], g_ref[...]) * (1.0 + mod_ref[0, 1:2, :]) + mod_ref[0, 0:1, :]
    qkv = _dot(u, w_ref[...])
    c, slo, shi = c_ref[...], slo_ref[...], shi_ref[...]
    n_q = GQA_HEADS * GQA_HEAD
    n_kv = GQA_KV_HEADS * GQA_HEAD
    for hd in range(GQA_HEADS):
        x = _rms(qkv[:, hd * LANES:(hd + 1) * LANES], qn_ref[...])
        q_ref[:, hd * LANES:(hd + 1) * LANES] = (_rope(x, c, slo, shi, GQA_HEAD // 2) * scale).astype(BF16)
    for hd in range(GQA_KV_HEADS):
        x = _rms(qkv[:, n_q + hd * LANES:n_q + (hd + 1) * LANES], kn_ref[...])
        k_ref[:, hd * LANES:(hd + 1) * LANES] = _rope(x, c, slo, shi, GQA_HEAD // 2).astype(BF16)
    v_ref[...] = qkv[:, n_q + n_kv:].astype(BF16)


def gqa_layer(lay, h, mod, gain, w_qkv, q_norm, k_norm, w_o):
    d = lay.d
    tabs = rope_tables(lay, GQA_HEAD)
    n_q = GQA_HEADS * GQA_HEAD
    n_kv = GQA_KV_HEADS * GQA_HEAD
    q, k, v = pl.pallas_call(
        functools.partial(_gqa_qkv_body, float(GQA_HEAD) ** -0.5),
        grid=(lay.nblocks,),
        in_specs=[_rows(d), _full((1, d)), _modspec(lay), _full((d, n_q + 2 * n_kv)), _full((1, LANES)),
                  _full((1, LANES))] + [_posrows(lay)] * 3,
        out_specs=[_rows(n_q), _rows(n_kv), _rows(n_kv)],
        out_shape=[jax.ShapeDtypeStruct((lay.n, n_q), BF16), jax.ShapeDtypeStruct((lay.n, n_kv), BF16),
                   jax.ShapeDtypeStruct((lay.n, n_kv), BF16)],
        compiler_params=_cparams(("parallel",)),
        name="gqa_qkv",
    )(h, gain.reshape(1, d), mod, w_qkv.astype(BF16), q_norm.reshape(1, LANES), k_norm.reshape(1, LANES), *tabs)
    o = attention(lay, q, k, v, GQA_HEADS, GQA_HEADS // GQA_KV_HEADS, GQA_HEAD, GQA_HEAD)
    return out_proj(lay, o, w_o, jnp.zeros((d,), F32), h, mod)


def _row_tiles(d):
    return d // LANES


def _store_row_tiles(ref, x):
    rows, d = x.shape
    nt = _row_tiles(d)
    for j in range(nt):
        ref[pl.ds(j, rows, stride=nt), :] = x[:, j * LANES:(j + 1) * LANES]


def _load_row_tiles(ref, rows, d):
    nt = _row_tiles(d)
    return jnp.concatenate([ref[pl.ds(j, rows, stride=nt), :] for j in range(nt)], axis=-1)


def _ffn_prep_body(h_ref, g_ref, mod_ref, wr_ref, br_ref, tri_ref, v_ref, idx_ref, wt_ref, rank_ref, cnt_ref,
                   carry_ref):
    @pl.when(pl.program_id(0) == 0)
    def _():
        carry_ref[...] = jnp.zeros_like(carry_ref)

    v = _rms(h_ref[...], g_ref[...]) * (1.0 + mod_ref[0, 4:5, :]) + mod_ref[0, 3:4, :]
    _store_row_tiles(v_ref, v)
    logit = _dot(v, wr_ref[...]) + br_ref[...]
    lane = lax.broadcasted_iota(jnp.int32, logit.shape, 1).astype(F32)
    idx = jnp.zeros_like(logit)
    ex = jnp.zeros_like(logit)
    den = 0.0
    m0 = None
    hot = []
    for kx in range(TOP_K):
        m = jnp.max(logit, axis=-1, keepdims=True)
        am = jnp.min(jnp.where(logit == m, lane, float(LANES)), axis=-1, keepdims=True)
        if kx == 0:
            m0 = m
        e = jnp.exp(m - m0)
        den = den + e
        idx = jnp.where(lane == kx, am, idx)
        ex = jnp.where(lane == kx, e, ex)
        hot.append(lane == am)
        logit = jnp.where(hot[kx], -jnp.inf, logit)
    idx_ref[...] = idx.astype(jnp.int32)
    wt_ref[...] = ex / den
    chosen = jnp.where(hot[0] | hot[1] | hot[2] | hot[3], 1.0, 0.0)
    before = carry_ref[...] + jnp.dot(tri_ref[...], chosen.astype(BF16), preferred_element_type=F32)
    rank = jnp.zeros_like(logit)
    for kx in range(TOP_K):
        rk = jnp.sum(jnp.where(hot[kx], before, 0.0), axis=-1, keepdims=True)
        rank = jnp.where(lane == kx, rk, rank)
    rank_ref[...] = rank.astype(jnp.int32)
    carry_ref[...] = carry_ref[...] + jnp.sum(chosen, axis=0, keepdims=True)
    cnt_ref[...] = carry_ref[...]


def ffn_prep(lay, h, gain, mod, w_r, b_r):
    d = lay.d
    ne = w_r.shape[1]
    wr = jnp.pad(w_r, ((0, 0), (0, LANES - ne))).astype(BF16)
    br = jnp.concatenate([b_r, jnp.full((LANES - ne,), -1e30, F32)]).reshape(1, LANES)
    tri = jnp.asarray(np.tril(np.ones((TM, TM), np.float32), -1)).astype(BF16)
    return pl.pallas_call(
        _ffn_prep_body,
        grid=(lay.nblocks,),
        in_specs=[_rows(d), _full((1, d)), _modspec(lay), _full((d, LANES)), _full((1, LANES)), _full((TM, TM))],
        out_specs=[_rows(LANES, TM * _row_tiles(d)), _rows(LANES), _rows(LANES), _rows(LANES), _full((1, LANES))],
        out_shape=[jax.ShapeDtypeStruct((lay.n * _row_tiles(d), LANES), F32),
                   jax.ShapeDtypeStruct((lay.n, LANES), jnp.int32),
                   jax.ShapeDtypeStruct((lay.n, LANES), F32), jax.ShapeDtypeStruct((lay.n, LANES), jnp.int32),
                   jax.ShapeDtypeStruct((1, LANES), F32)],
        scratch_shapes=[pltpu.VMEM((1, LANES), F32)],
        compiler_params=_cparams(("arbitrary",)),
        name="ffn_prep",
    )(h, gain.reshape(1, d), mod, wr, br, tri)


def _row_at(ref, row, nt):
    return ref.at[pl.ds(pl.multiple_of(row * nt, nt), nt)]


def _dispatch_body(nt, ne, pad_ref, dest_ref, v_ref, xs_ref, zero_ref, sem):
    @pl.when(pl.program_id(0) == 0)
    def _():
        zero_ref[...] = jnp.zeros_like(zero_ref)
        for e in range(ne):
            def issue_zero(i, carry, e=e):
                pltpu.make_async_copy(zero_ref, _row_at(xs_ref, pad_ref[0, e] + i, nt), sem).start()
                return carry

            lax.fori_loop(0, pad_ref[1, e], issue_zero, 0)
        for e in range(ne):
            def drain_zero(i, carry):
                pltpu.make_async_copy(zero_ref, _row_at(xs_ref, 0, nt), sem).wait()
                return carry

            lax.fori_loop(0, pad_ref[1, e], drain_zero, 0)

    def issue(t, carry):
        for kx in range(TOP_K):
            pltpu.make_async_copy(_row_at(v_ref, t, nt), _row_at(xs_ref, dest_ref[0, 0, t * TOP_K + kx], nt),
                                  sem).start(priority=kx % 2)
        return carry

    lax.fori_loop(0, TM, issue, 0)

    def drain(t, carry):
        for kx in range(TOP_K):
            pltpu.make_async_copy(_row_at(v_ref, 0, nt), _row_at(xs_ref, 0, nt), sem).wait()
        return carry

    lax.fori_loop(0, TM, drain, 0)


def dispatch(lay, v, dest3, pad_info, n_rows):
    nt = _row_tiles(lay.d)
    ne = pad_info.shape[1]
    grid_spec = pltpu.PrefetchScalarGridSpec(
        num_scalar_prefetch=1,
        grid=(lay.nblocks,),
        in_specs=[pl.BlockSpec((1, 1, TM * TOP_K), lambda i, pad: (i, 0, 0), memory_space=pltpu.SMEM),
                  pl.BlockSpec((TM * nt, LANES), lambda i, pad: (i, 0))],
        out_specs=pl.BlockSpec(memory_space=pl.ANY),
        scratch_shapes=[pltpu.VMEM((nt, LANES), v.dtype), pltpu.SemaphoreType.DMA],
    )
    return pl.pallas_call(
        functools.partial(_dispatch_body, nt, ne),
        grid_spec=grid_spec,
        out_shape=jax.ShapeDtypeStruct((n_rows * nt, LANES), v.dtype),
        compiler_params=_cparams(("arbitrary",)),
        name="moe_dispatch",
    )(pad_info, dest3, v)


def _experts_body(be_ref, bv_ref, x_ref, wg_ref, bg_ref, wu_ref, bu_ref, wd_ref, bd_ref, y_ref):
    i = pl.program_id(0)

    @pl.when(bv_ref[i] > 0)
    def _():
        x = _load_row_tiles(x_ref, EXPERT_ROWS, wg_ref.shape[1]).astype(BF16)
        gt = jnp.minimum(jnp.dot(x, wg_ref[0], preferred_element_type=F32) + bg_ref[0], SWIGLU_LIMIT)
        up = jnp.clip(jnp.dot(x, wu_ref[0], preferred_element_type=F32) + bu_ref[0], -SWIGLU_LIMIT, SWIGLU_LIMIT)
        act = gt * _sigmoid(SWIGLU_ALPHA * gt) * (up + 1.0)
        _store_row_tiles(y_ref, _dot(act, wd_ref[0]) + bd_ref[0])

    @pl.when(bv_ref[i] == 0)
    def _():
        y_ref[...] = jnp.zeros_like(y_ref)


def experts(layer, xs, blk_e, blk_valid, wg, bg, wu, bu, wd, bd):
    nl, ne, d, ff = wg.shape
    nt = _row_tiles(d)
    n_rows = xs.shape[0] // nt
    r = EXPERT_ROWS
    grid_spec = pltpu.PrefetchScalarGridSpec(
        num_scalar_prefetch=2,
        grid=(n_rows // r,),
        in_specs=[pl.BlockSpec((r * nt, LANES), lambda i, be, bv: (jnp.where(bv[i] > 0, i, 0), 0)),
                  pl.BlockSpec((None, 1, d, ff), lambda i, be, bv: (layer, be[i], 0, 0)),
                  pl.BlockSpec((None, 1, 1, ff), lambda i, be, bv: (layer, be[i], 0, 0)),
                  pl.BlockSpec((None, 1, d, ff), lambda i, be, bv: (layer, be[i], 0, 0)),
                  pl.BlockSpec((None, 1, 1, ff), lambda i, be, bv: (layer, be[i], 0, 0)),
                  pl.BlockSpec((None, 1, ff, d), lambda i, be, bv: (layer, be[i], 0, 0)),
                  pl.BlockSpec((None, 1, 1, d), lambda i, be, bv: (layer, be[i], 0, 0))],
        out_specs=pl.BlockSpec((r * nt, LANES), lambda i, be, bv: (i, 0)),
    )
    return pl.pallas_call(
        _experts_body,
        grid_spec=grid_spec,
        out_shape=jax.ShapeDtypeStruct((n_rows * nt, LANES), F32),
        compiler_params=_cparams(("arbitrary",)),
        name="experts",
    )(blk_e, blk_valid, xs, wg, bg.reshape(nl, ne, 1, ff), wu, bu.reshape(nl, ne, 1, ff), wd,
      bd.reshape(nl, ne, 1, d))


def _combine_body(nt, nblocks, dest_ref, dnext_ref, wt_ref, h_ref, mod_ref, y_ref, out_ref, ybuf, fbuf, sems):
    i = pl.program_id(0)
    slot = i % 2

    def gather(dref, sl):
        def issue(t, carry):
            for kx in range(TOP_K):
                pltpu.make_async_copy(_row_at(y_ref, dref[0, 0, t * TOP_K + kx], nt), _row_at(ybuf.at[sl, kx], t, nt),
                                      sems.at[sl]).start(priority=kx % 2)
            return carry

        lax.fori_loop(0, TM, issue, 0)

    @pl.when(i == 0)
    def _():
        gather(dest_ref, 0)

    @pl.when(i + 1 < nblocks)
    def _():
        gather(dnext_ref, 1 - slot)

    def drain(t, carry):
        for kx in range(TOP_K):
            pltpu.make_async_copy(_row_at(y_ref, 0, nt), _row_at(ybuf.at[slot, kx], 0, nt), sems.at[slot]).wait()
        return carry

    lax.fori_loop(0, TM, drain, 0)

    def weigh(t, carry):
        base = pl.multiple_of(t * nt, nt)
        acc = wt_ref[0, 0, t * TOP_K] * ybuf[slot, 0, pl.ds(base, nt), :]
        for kx in range(1, TOP_K):
            acc = acc + wt_ref[0, 0, t * TOP_K + kx] * ybuf[slot, kx, pl.ds(base, nt), :]
        fbuf[pl.ds(base, nt), :] = acc
        return carry

    lax.fori_loop(0, TM, weigh, 0, unroll=4)
    out_ref[...] = h_ref[...] + mod_ref[0, 5:6, :] * _load_row_tiles(fbuf, TM, h_ref.shape[1])


def combine(lay, h, y, dest3, wts3, mod):
    d = lay.d
    nt = _row_tiles(d)
    nb = lay.nblocks
    smem = lambda f: pl.BlockSpec((1, 1, TM * TOP_K), f, memory_space=pltpu.SMEM)
    return pl.pallas_call(
        functools.partial(_combine_body, nt, nb),
        grid=(nb,),
        in_specs=[smem(lambda i: (i, 0, 0)), smem(lambda i: (jnp.minimum(i + 1, nb - 1), 0, 0)),
                  smem(lambda i: (i, 0, 0)), _rows(d), _modspec(lay), pl.BlockSpec(memory_space=pl.ANY)],
        out_specs=_rows(d),
        out_shape=jax.ShapeDtypeStruct((lay.n, d), F32),
        scratch_shapes=[pltpu.VMEM((2, TOP_K, TM * nt, LANES), F32), pltpu.VMEM((TM * nt, LANES), F32),
                        pltpu.SemaphoreType.DMA((2,))],
        compiler_params=_cparams(("arbitrary",)),
        name="moe_combine",
    )(dest3, dest3, wts3, h, mod, y)


def moe_layer(lay, layer, h, mod, gain, w_r, b_r, wg, bg, wu, bu, wd, bd):
    ne = w_r.shape[1]
    r = EXPERT_ROWS
    v, idx, wts, rank, cnt = ffn_prep(lay, h, gain, mod, w_r, b_r)
    n_slot = lay.n * TOP_K
    counts = cnt[0, :ne].astype(jnp.int32)
    padded = (counts + r - 1) // r * r
    pad_end = jnp.cumsum(padded)
    pad_start = pad_end - padded
    dest = pad_start[idx[:, :TOP_K]] + rank[:, :TOP_K]
    dest3 = dest.reshape(lay.nblocks, 1, TM * TOP_K)
    n_rows = n_slot + ne * r
    blk_start = jnp.arange(n_rows // r, dtype=jnp.int32) * r
    blk_e = jnp.minimum(jnp.sum((blk_start[:, None] >= pad_end[None, :]).astype(jnp.int32), axis=1), ne - 1)
    blk_valid = (blk_start < pad_end[-1]).astype(jnp.int32)
    pad_info = jnp.stack([jnp.concatenate([pad_start + counts, pad_end[-1:]]),
                          jnp.concatenate([padded - counts, n_rows - pad_end[-1:]])]).astype(jnp.int32)
    xs = dispatch(lay, v, dest3, pad_info, n_rows)
    y = experts(layer, xs, blk_e, blk_valid, wg, bg, wu, bu, wd, bd)
    wts3 = wts[:, :TOP_K].reshape(lay.nblocks, 1, TM * TOP_K)
    return combine(lay, h, y, dest3, wts3, mod)


def kernel(x, c, ctx, c_ctx, mod_w, mod_b, norm_mix, norm_ffn, router_w, router_b, exp_w_gate, exp_b_gate, exp_w_up, exp_b_up, exp_w_down, exp_b_down, rw_mu, rw_wr, rw_wk, rw_wv, rw_wo, rw_w0, rw_w1, rw_w2, rw_a0, rw_a1, rw_a2, rw_g1, rw_g2, rw_kk, rw_ka, rw_rk, rw_lnx_g, rw_lnx_b, mla_w_down, mla_qa_norm, mla_w_qb, mla_kv_norm, mla_w_kvb, mla_qn_nope, mla_qn_rope, mla_kn_nope, mla_kn_rope, mla_wo, fn_w, fn_b, gqa_w_qkv, gqa_q_norm, gqa_k_norm, gqa_wo):
    bsz, seq, d = x.shape
    ctx_len = ctx.shape[1]
    depth = mod_w.shape[0]
    lay = Layout(bsz, ctx_len, seq, d)
    assert bsz < 16 and d // PAIR * PAIR == d

    cin = jnp.zeros((16, d), F32).at[:bsz].set(c).at[bsz].set(c_ctx)
    mod_all = modulation(cin, mod_w, mod_b).reshape(depth, 16, 6, d)

    h = jnp.concatenate([ctx, x], axis=1).reshape(lay.n, d)
    wg_all, wu_all, wd_all = exp_w_gate.astype(BF16), exp_w_up.astype(BF16), exp_w_down.astype(BF16)
    n_mixers = 4
    for i in range(depth):
        m, j = i % n_mixers, i // n_mixers
        mod = mod_all[i]
        if m == 0:
            h = rwkv_layer(lay, h, mod, norm_mix[i], rw_mu[j], rw_wr[j], rw_wk[j], rw_wv[j], rw_wo[j], rw_w0[j],
                           rw_w1[j], rw_w2[j], rw_a0[j], rw_a1[j], rw_a2[j], rw_g1[j], rw_g2[j], rw_kk[j],
                           rw_ka[j], rw_rk[j], rw_lnx_g[j], rw_lnx_b[j])
        elif m == 1:
            h = mla_layer(lay, h, mod, norm_mix[i], mla_w_down[j], mla_qa_norm[j], mla_w_qb[j], mla_kv_norm[j],
                          mla_w_kvb[j], mla_qn_nope[j], mla_qn_rope[j], mla_kn_nope[j], mla_kn_rope[j], mla_wo[j])
        elif m == 2:
            h = fourier_layer(lay, h, mod, norm_mix[i], fn_w[j], fn_b[j])
        else:
            h = gqa_layer(lay, h, mod, norm_mix[i], gqa_w_qkv[j], gqa_q_norm[j], gqa_k_norm[j], gqa_wo[j])
        h = moe_layer(lay, i, h, mod, norm_ffn[i], router_w[i], router_b[i], wg_all, exp_b_gate, wu_all, exp_b_up,
                      wd_all, exp_b_down)
    return h.reshape(bsz, lay.tt, d)[:, ctx_len:, :]
```

```python
import functools
import math

import numpy as np
import jax
import jax.numpy as jnp
from jax import lax
from jax.experimental import pallas as pl
from jax.experimental.pallas import tpu as pltpu

F32 = jnp.float32
BF16 = jnp.bfloat16

LANES = 128
SUBLANES = 8
VMEM_LIMIT_BYTES = 56 * 1024 * 1024

GRID_W = 64
ROPE_THETA = 10000.0
NORM_EPS = 1e-6
RW_HEAD = 64
RW_GN_EPS = 64e-5
MLA_HEADS = 16
MLA_Q_RANK = 512
MLA_KV_RANK = 512
MLA_NOPE = 128
MLA_ROPE = 64
MLA_V = 128
FN_GROUPS = 8
GQA_HEADS = 16
GQA_KV_HEADS = 4
GQA_HEAD = 128
TOP_K = 4
SWIGLU_LIMIT = 7.0
SWIGLU_ALPHA = 1.702

TM = 256
WKV_CHUNK = 64
WKV_GROUP = 16
ATTN_ROWS = 256
EXPERT_ROWS = 256
PAIR = 2 * RW_HEAD


def _cparams(sem):
    return pltpu.CompilerParams(dimension_semantics=sem, vmem_limit_bytes=VMEM_LIMIT_BYTES)


def _dot(a, b):
    return jnp.dot(a.astype(BF16), b.astype(BF16), preferred_element_type=F32)


def _dot_nt(a, b):
    return lax.dot_general(a.astype(BF16), b.astype(BF16), (((1,), (1,)), ((), ())),
                           preferred_element_type=F32)


def _rms(x, gain):
    return x * lax.rsqrt(jnp.mean(x * x, axis=-1, keepdims=True) + NORM_EPS) * gain


def _sigmoid(x):
    return 1.0 / (1.0 + jnp.exp(-x))


class Layout:
    def __init__(self, batch, ctx_len, seq, d):
        self.b, self.ctx, self.seq, self.d = batch, ctx_len, seq, d
        self.tt = ctx_len + seq
        self.n = batch * self.tt
        assert ctx_len % TM == 0 and seq % TM == 0
        self.nb = self.tt // TM
        self.cb = ctx_len // TM
        self.nblocks = self.n // TM

    def seg(self, i):
        return jnp.where(i % self.nb < self.cb, self.b, i // self.nb)


def _rows(ncols, tm=TM):
    return pl.BlockSpec((tm, ncols), lambda i: (i, 0))


def _full(shape):
    nd = len(shape)
    return pl.BlockSpec(shape, lambda i: (0,) * nd)


def _modspec(lay):
    return pl.BlockSpec((1, 6, lay.d), lambda i: (lay.seg(i), 0, 0))


def _posrows(lay, ncols=LANES):
    return pl.BlockSpec((TM, ncols), lambda i: (i % lay.nb, 0))


def _pairs(tm=TM):
    return pl.BlockSpec((None, tm, PAIR), lambda i: (0, i, 0))


def _mod_body(c_ref, w_ref, b_ref, o_ref):
    c = c_ref[...]
    s = c * _sigmoid(c)
    o_ref[0] = _dot(s, w_ref[0]) + b_ref[0]


def modulation(cin, mod_w, mod_b):
    depth, d, n6 = mod_w.shape
    tn = 1024
    return pl.pallas_call(
        _mod_body,
        grid=(depth, n6 // tn),
        in_specs=[pl.BlockSpec((16, d), lambda l, j: (0, 0)),
                  pl.BlockSpec((1, d, tn), lambda l, j: (l, 0, j)),
                  pl.BlockSpec((1, 1, tn), lambda l, j: (l, 0, j))],
        out_specs=pl.BlockSpec((1, 16, tn), lambda l, j: (l, 0, j)),
        out_shape=jax.ShapeDtypeStruct((depth, 16, n6), F32),
        compiler_params=_cparams(("parallel", "parallel")),
        name="modulation",
    )(cin, mod_w, mod_b.reshape(depth, 1, n6))


def _rw_prep_body(lay, h_ref, hp_ref, hn_ref, g_ref, mod_ref, mu_ref, *outs):
    j = pl.program_id(0) % lay.nb
    seg_start = jnp.logical_or(j == 0, j == lay.cb)
    seg_end = jnp.logical_or(j == lay.cb - 1, j == lay.nb - 1)
    gain = g_ref[...]
    shift, scale = mod_ref[0, 0:1, :], mod_ref[0, 1:2, :]

    def norm_mod(x):
        return _rms(x, gain) * (1.0 + scale) + shift

    u = norm_mod(h_ref[...])
    prev = jnp.where(seg_start, 0.0, norm_mod(hp_ref[SUBLANES - 1:SUBLANES, :]))
    nxt = jnp.where(seg_end, 0.0, norm_mod(hn_ref[0:1, :]))
    row = lax.broadcasted_iota(jnp.int32, u.shape, 0)
    up = jnp.where(row == 0, prev, pltpu.roll(u, 1, 0))
    un = jnp.where(row == TM - 1, nxt, pltpu.roll(u, TM - 1, 0))
    du = 0.5 * (up + un) - u
    for n, o_ref in enumerate(outs):
        o_ref[...] = (u + du * mu_ref[n:n + 1, :]).astype(BF16)


def rw_prep(lay, h, gain, mod, mu):
    d = lay.d
    r8 = TM // SUBLANES
    last8 = lay.n // SUBLANES - 1
    mu8 = jnp.concatenate([mu, jnp.zeros((2, d), F32)], axis=0)
    return pl.pallas_call(
        functools.partial(_rw_prep_body, lay),
        grid=(lay.nblocks,),
        in_specs=[_rows(d),
                  pl.BlockSpec((SUBLANES, d), lambda i: (jnp.maximum(i * r8 - 1, 0), 0)),
                  pl.BlockSpec((SUBLANES, d), lambda i: (jnp.minimum((i + 1) * r8, last8), 0)),
                  _full((1, d)), _modspec(lay), _full((8, d))],
        out_specs=[_rows(d)] * 6,
        out_shape=[jax.ShapeDtypeStruct((lay.n, d), BF16)] * 6,
        compiler_params=_cparams(("parallel",)),
        name="rw_prep",
    )(h, h, h, gain.reshape(1, d), mod, mu8)


def _linear_pairs_body(npairs, x_ref, w_ref, o_ref):
    y = _dot(x_ref[...], w_ref[...])
    for p in range(npairs):
        o_ref[p] = y[:, p * PAIR:(p + 1) * PAIR]


def linear_pairs(lay, x, w):
    d = lay.d
    npairs = d // PAIR
    return pl.pallas_call(
        functools.partial(_linear_pairs_body, npairs),
        grid=(lay.nblocks,),
        in_specs=[_rows(d), _full((d, d))],
        out_specs=pl.BlockSpec((npairs, TM, PAIR), lambda i: (0, i, 0)),
        out_shape=jax.ShapeDtypeStruct((npairs, lay.n, PAIR), F32),
        compiler_params=_cparams(("parallel",)),
        name="rw_linear",
    )(x, w)


def _rw_lora_body(npairs, xw_ref, xa_ref, xg_ref, w1_ref, w2_ref, a1_ref, a2_ref, g1_ref, g2_ref,
                  w0_ref, a0_ref, lw_ref, ag_ref, g_ref):
    hw = jnp.tanh(_dot(xw_ref[...], w1_ref[...]))
    ha = _dot(xa_ref[...], a1_ref[...])
    hg = _sigmoid(_dot(xg_ref[...], g1_ref[...]))
    rw = w1_ref.shape[1] // 2
    rg = g1_ref.shape[1] // 2
    for dr in range(2):
        z = w0_ref[dr:dr + 1, :] + _dot(hw[:, dr * rw:(dr + 1) * rw], w2_ref[dr])
        lw = -math.exp(-0.5) * _sigmoid(z)
        a = _sigmoid(a0_ref[dr:dr + 1, :] + _dot(ha[:, dr * rw:(dr + 1) * rw], a2_ref[dr]))
        g = _dot(hg[:, dr * rg:(dr + 1) * rg], g2_ref[dr])
        for p in range(npairs):
            sl = slice(p * PAIR, (p + 1) * PAIR)
            lw_ref[dr, p] = lw[:, sl]
            ag_ref[dr, p] = a[:, sl]
            g_ref[dr, p] = g[:, sl]


def _pad_lora(w1, w2):
    r = w1.shape[2]
    rp = -(-r // LANES) * LANES
    w1p = jnp.pad(w1, ((0, 0), (0, 0), (0, rp - r)))
    w1p = jnp.concatenate([w1p[0], w1p[1]], axis=1).astype(BF16)
    w2p = jnp.pad(w2, ((0, 0), (0, rp - r), (0, 0))).astype(BF16)
    return w1p, w2p


def rw_lora(lay, xw, xa, xg, w0, w1, w2, a0, a1, a2, g1, g2):
    d = lay.d
    npairs = d // PAIR
    w1p, w2p = _pad_lora(w1, w2)
    a1p, a2p = _pad_lora(a1, a2)
    g1p, g2p = _pad_lora(g1, g2)
    ospec = pl.BlockSpec((2, npairs, TM, PAIR), lambda i: (0, 0, i, 0))
    oshape = jax.ShapeDtypeStruct((2, npairs, lay.n, PAIR), F32)
    return pl.pallas_call(
        functools.partial(_rw_lora_body, npairs),
        grid=(lay.nblocks,),
        in_specs=[_rows(d)] * 3 + [_full(w1p.shape), _full(w2p.shape), _full(a1p.shape), _full(a2p.shape),
                                    _full(g1p.shape), _full(g2p.shape), _full((2, d)), _full((2, d))],
        out_specs=[ospec] * 3,
        out_shape=[oshape] * 3,
        compiler_params=_cparams(("parallel",)),
        name="rw_lora",
    )(xw, xa, xg, w1p, w2p, a1p, a2p, g1p, g2p, w0, a0)


def _bd(z, head0):
    return jnp.concatenate([jnp.where(head0, z, 0.0), jnp.where(head0, 0.0, z)], axis=0)


def _head_sum(x, head0):
    s0 = jnp.sum(jnp.where(head0, x, 0.0), axis=-1, keepdims=True)
    s1 = jnp.sum(jnp.where(head0, 0.0, x), axis=-1, keepdims=True)
    return jnp.where(head0, s0, s1)


def _wkv_group_chunk(rev, tiles, par, hts):
    c = WKV_CHUNK
    rng = range(len(tiles))
    t = lax.broadcasted_iota(jnp.int32, (c, PAIR), 0)
    lane = lax.broadcasted_iota(jnp.int32, (c, PAIR), 1)
    s = lane % RW_HEAD
    head0 = lane < RW_HEAD
    strict = (s > t) if rev else (s < t)
    incl = (s >= t) if rev else (s <= t)
    eye = jnp.where(s == t, 1.0, 0.0)
    rr = lax.broadcasted_iota(jnp.int32, (PAIR, PAIR), 0) // RW_HEAD
    cc = lax.broadcasted_iota(jnp.int32, (PAIR, PAIR), 1) // RW_HEAD
    same_head = rr == cc

    def prep(q):
        r, k, v, lw, ag, _ = tiles[q]
        kk_p, ka_p = par[q][0], par[q][1]
        kkf = k * kk_p
        kk = kkf / jnp.maximum(jnp.sqrt(_head_sum(kkf * kkf, head0)), 1e-12)
        kd = k * (1.0 + (ag - 1.0) * ka_p)
        bvec = kk * ag
        cs = lw
        for sh in (1, 2, 4, 8, 16, 32):
            if rev:
                cs = cs + jnp.where(t + sh < c, pltpu.roll(cs, c - sh, 0), 0.0)
            else:
                cs = cs + jnp.where(t >= sh, pltpu.roll(cs, sh, 0), 0.0)
        cl = cs[0:1, :] if rev else cs[c - 1:c, :]
        e_neg = jnp.exp(-cs)
        e_end = jnp.exp(cl - cs)
        ar = jnp.concatenate([-kk * jnp.exp(cs - lw), r * jnp.exp(cs)], axis=0).astype(BF16)
        bk = jnp.concatenate([_bd(bvec * e_neg, head0), _bd(kd * e_neg, head0)], axis=0).astype(BF16)
        bk_end = jnp.concatenate([bvec * e_end, kd * e_end], axis=0).astype(BF16)
        return dict(ar=ar, bk=bk, bk_end=bk_end, kd=kd, decay=jnp.exp(cl), vbd=_bd(v, head0).astype(BF16))

    st = [prep(q) for q in rng]
    p = [_dot_nt(st[q]["ar"], st[q]["bk"]) for q in rng]
    arh = [_dot_nt(st[q]["ar"], hts[q]) for q in rng]
    l_ab = [jnp.where(strict, p[q][0:c, 0:PAIR], 0.0) for q in rng]
    l_ak = [jnp.where(strict, p[q][0:c, PAIR:2 * PAIR], 0.0) for q in rng]
    m_r = [jnp.concatenate([jnp.where(incl, p[q][c:2 * c, 0:PAIR], 0.0),
                            jnp.where(incl, p[q][c:2 * c, PAIR:2 * PAIR], 0.0)], axis=1).astype(BF16) for q in rng]
    x = [arh[q][0:c] + _dot(l_ak[q], st[q]["vbd"]) for q in rng]
    tinv = [eye + l_ab[q] for q in rng]
    pw = l_ab
    for _ in range(5):
        pw = [_dot(pw[q], _bd(pw[q], head0)) for q in rng]
        tinv = [tinv[q] + _dot(tinv[q], _bd(pw[q], head0)) for q in rng]
    u = [_dot(tinv[q], _bd(x[q], head0)) for q in rng]
    y = [arh[q][c:2 * c] + _dot(m_r[q], jnp.concatenate([_bd(u[q], head0).astype(BF16), st[q]["vbd"]], axis=0))
         for q in rng]
    upd = [_dot(jnp.concatenate([u[q], tiles[q][2]], axis=0).T, st[q]["bk_end"]) for q in rng]
    ht_new = [hts[q] * st[q]["decay"] + jnp.where(same_head, upd[q], 0.0) for q in rng]

    outs = []
    inv_n = 1.0 / RW_HEAD
    for q in rng:
        r, _, v, _, _, g = tiles[q]
        rk_p, lg_p, lb_p = par[q][2], par[q][3], par[q][4]
        mu = _head_sum(y[q], head0) * inv_n
        yc = y[q] - mu
        var = _head_sum(yc * yc, head0) * inv_n
        bonus = _head_sum(r * st[q]["kd"] * rk_p, head0) * v
        outs.append((yc * lax.rsqrt(var + RW_GN_EPS) * lg_p + lb_p + bonus) * g)
    return outs, ht_new


def _wkv_body(rev, npairs, r_ref, k_ref, v_ref, lw_ref, ag_ref, g_ref, kk_ref, ka_ref, rk_ref, lg_ref, lb_ref,
              o_ref, ht_ref):
    @pl.when(pl.program_id(1) == 0)
    def _():
        ht_ref[...] = jnp.zeros_like(ht_ref)

    def group(gi, carry):
        ps = [gi * WKV_GROUP + q for q in range(WKV_GROUP)]
        tiles = [(r_ref[p], k_ref[p], v_ref[p], lw_ref[0, p], ag_ref[0, p], g_ref[0, p]) for p in ps]
        par = [(kk_ref[p], ka_ref[p], rk_ref[p], lg_ref[p], lb_ref[p]) for p in ps]
        outs, hts = _wkv_group_chunk(rev, tiles, par, [ht_ref[p] for p in ps])
        for q, p in enumerate(ps):
            o_ref[p] = outs[q]
            ht_ref[p] = hts[q]
        return carry

    lax.fori_loop(0, npairs // WKV_GROUP, group, 0)


def wkv(lay, rev, r, k, v, lw, ag, g, params):
    npairs = lay.d // PAIR
    c = WKV_CHUNK
    nch = lay.tt // c
    cch = lay.ctx // c
    dr = 1 if rev else 0

    def rowblk(b, ci):
        if rev:
            return b * nch + jnp.where(ci < cch, cch - 1 - ci, nch + cch - 1 - ci)
        return b * nch + ci

    tok = pl.BlockSpec((npairs, c, PAIR), lambda b, ci: (0, rowblk(b, ci), 0))
    tokd = pl.BlockSpec((1, npairs, c, PAIR), lambda b, ci: (dr, 0, rowblk(b, ci), 0))
    par = pl.BlockSpec((npairs, 1, PAIR), lambda b, ci: (0, 0, 0))
    return pl.pallas_call(
        functools.partial(_wkv_body, rev, npairs),
        grid=(lay.b, nch),
        in_specs=[tok, tok, tok, tokd, tokd, tokd] + [par] * 5,
        out_specs=tok,
        out_shape=jax.ShapeDtypeStruct((npairs, lay.n, PAIR), F32),
        scratch_shapes=[pltpu.VMEM((npairs, PAIR, PAIR), F32)],
        compiler_params=_cparams(("parallel", "arbitrary")),
        name="wkv_rev" if rev else "wkv_fwd",
    )(r, k, v, lw, ag, g, *params)


def _rw_out_body(npairs, o0_ref, o1_ref, h_ref, mod_ref, w_ref, out_ref):
    acc = jnp.concatenate([o0_ref[p] + o1_ref[p] for p in range(npairs)], axis=-1)
    out_ref[...] = h_ref[...] + mod_ref[0, 2:3, :] * _dot(acc, w_ref[...])


def rw_out(lay, o0, o1, h, mod, w):
    d = lay.d
    npairs = d // PAIR
    pm = pl.BlockSpec((npairs, TM, PAIR), lambda i: (0, i, 0))
    return pl.pallas_call(
        functools.partial(_rw_out_body, npairs),
        grid=(lay.nblocks,),
        in_specs=[pm, pm, _rows(d), _modspec(lay), _full((d, d))],
        out_specs=_rows(d),
        out_shape=jax.ShapeDtypeStruct((lay.n, d), F32),
        compiler_params=_cparams(("parallel",)),
        name="rw_out",
    )(o0, o1, h, mod, w)


def rwkv_layer(lay, h, mod, gain, mu, w_r, w_k, w_v, w_o, w0, w1, w2, a0, a1, a2, g1, g2, k_k, k_a, r_k,
               lnx_g, lnx_b):
    npairs = lay.d // PAIR
    xr, xw, xk, xv, xa, xg = rw_prep(lay, h, gain, mod, mu)
    r = linear_pairs(lay, xr, w_r.astype(BF16))
    k = linear_pairs(lay, xk, w_k.astype(BF16))
    v = linear_pairs(lay, xv, w_v.astype(BF16))
    lw, ag, g = rw_lora(lay, xw, xa, xg, w0, w1, w2, a0, a1, a2, g1, g2)
    params = [t.reshape(npairs, 1, PAIR) for t in (k_k, k_a, r_k.reshape(-1), lnx_g, lnx_b)]
    o0 = wkv(lay, False, r, k, v, lw, ag, g, params)
    o1 = wkv(lay, True, r, k, v, lw, ag, g, params)
    return rw_out(lay, o0, o1, h, mod, w_o.astype(BF16))


def rope_tables(lay, rot_dim):
    quarter = rot_dim // 4
    half = rot_dim // 2
    t = np.arange(lay.seq)
    inv = ROPE_THETA ** (-np.arange(quarter, dtype=np.float32) / quarter)
    ang = np.concatenate([(t // GRID_W)[:, None].astype(np.float32) * inv,
                          (t % GRID_W)[:, None].astype(np.float32) * inv], axis=-1)
    ang = jnp.asarray(ang, F32)
    cos, sin = jnp.cos(ang), jnp.sin(ang)
    pad = LANES - rot_dim
    zer = jnp.zeros((lay.seq, half), F32)
    c = jnp.concatenate([cos, cos, jnp.ones((lay.seq, pad), F32)], axis=-1)
    s_lo = jnp.concatenate([-sin, zer, jnp.zeros((lay.seq, pad), F32)], axis=-1)
    s_hi = jnp.concatenate([zer, sin, jnp.zeros((lay.seq, pad), F32)], axis=-1)

    def with_ctx(tab, fill):
        return jnp.concatenate([jnp.full((lay.ctx, LANES), fill, F32), tab], axis=0)

    return with_ctx(c, 1.0), with_ctx(s_lo, 0.0), with_ctx(s_hi, 0.0)


def _rope(x, c, s_lo, s_hi, half):
    return x * c + pltpu.roll(x, LANES - half, 1) * s_lo + pltpu.roll(x, half, 1) * s_hi


def _attn_rows(ctx, rc, row0, q_ref, k, v, o_ref):
    n = q_ref.shape[0] // rc
    ss = []
    for c in range(n):
        s = _dot_nt(q_ref[c * rc:(c + 1) * rc, :], k)
        if row0 is not None and row0 + c * rc < ctx:
            qrow = row0 + c * rc + lax.broadcasted_iota(jnp.int32, s.shape, 0)
            kcol = lax.broadcasted_iota(jnp.int32, s.shape, 1)
            s = jnp.where(jnp.logical_or(qrow >= ctx, kcol < ctx), s, -1e30)
        ss.append(s)
    ms = [jnp.max(s, axis=-1, keepdims=True) for s in ss]
    ps = [jnp.exp(ss[c] - ms[c]) for c in range(n)]
    ls = [jnp.sum(p, axis=-1, keepdims=True) for p in ps]
    for c in range(n):
        o_ref[c * rc:(c + 1) * rc, :] = (_dot(ps[c], v) / ls[c]).astype(o_ref.dtype)


def _attn_body(ctx, q_ref, k_ref, v_ref, o_ref):
    k = k_ref[...]
    v = v_ref[...]

    @pl.when(pl.program_id(2) == 0)
    def _():
        _attn_rows(ctx, ATTN_ROWS, 0, q_ref, k, v, o_ref)

    @pl.when(pl.program_id(2) != 0)
    def _():
        _attn_rows(ctx, ATTN_ROWS, None, q_ref, k, v, o_ref)


def attention(lay, q, k, v, n_heads, kv_group, dq, dv, tq=768):
    nq = lay.tt // tq
    assert lay.ctx <= tq and tq % ATTN_ROWS == 0
    k3 = k.reshape(lay.b, lay.tt, k.shape[1])
    v3 = v.reshape(lay.b, lay.tt, v.shape[1])
    return pl.pallas_call(
        functools.partial(_attn_body, lay.ctx),
        grid=(lay.b, n_heads, nq),
        in_specs=[pl.BlockSpec((tq, dq), lambda b, h, i: (b * nq + i, h)),
                  pl.BlockSpec((None, lay.tt, dq), lambda b, h, i: (b, 0, h // kv_group)),
                  pl.BlockSpec((None, lay.tt, dv), lambda b, h, i: (b, 0, h // kv_group))],
        out_specs=pl.BlockSpec((tq, dv), lambda b, h, i: (b * nq + i, h)),
        out_shape=jax.ShapeDtypeStruct((lay.n, n_heads * dv), BF16),
        compiler_params=_cparams(("parallel", "parallel", "arbitrary")),
        name="attention",
    )(q, k3, v3)


def _out_proj_body(x_ref, w_ref, b_ref, h_ref, mod_ref, out_ref):
    y = _dot(x_ref[...], w_ref[...]) + b_ref[...]
    out_ref[...] = h_ref[...] + mod_ref[0, 2:3, :] * y


def out_proj(lay, x, w, bias, h, mod):
    d = lay.d
    kdim = x.shape[1]
    return pl.pallas_call(
        _out_proj_body,
        grid=(lay.nblocks,),
        in_specs=[_rows(kdim), _full((kdim, d)), _full((1, d)), _rows(d), _modspec(lay)],
        out_specs=_rows(d),
        out_shape=jax.ShapeDtypeStruct((lay.n, d), F32),
        compiler_params=_cparams(("parallel",)),
        name="out_proj",
    )(x, w.astype(BF16), bias.reshape(1, d), h, mod)


def _mla_down_body(h_ref, g_ref, mod_ref, w_ref, qan_ref, kvn_ref, krn_ref, c_ref, slo_ref, shi_ref,
                   q_ref, ckv_ref, kr_ref):
    u = _rms(h_ref[...], g_ref[...]) * (1.0 + mod_ref[0, 1:2, :]) + mod_ref[0, 0:1, :]
    down = _dot(u, w_ref[...])
    q_ref[...] = _rms(down[:, 0:MLA_Q_RANK], qan_ref[...]).astype(BF16)
    ckv_ref[...] = _rms(down[:, MLA_Q_RANK:MLA_Q_RANK + MLA_KV_RANK], kvn_ref[...]).astype(BF16)
    kr = down[:, MLA_Q_RANK + MLA_KV_RANK:]
    ms = jnp.sum(kr * kr, axis=-1, keepdims=True) * (1.0 / MLA_ROPE)
    kr = kr * lax.rsqrt(ms + NORM_EPS) * krn_ref[...]
    kr_ref[...] = _rope(kr, c_ref[...], slo_ref[...], shi_ref[...], MLA_ROPE // 2).astype(BF16)


def _mla_q_body(scale, x_ref, w_ref, nn_ref, rn_ref, c_ref, slo_ref, shi_ref, q_ref):
    q = _dot(x_ref[...], w_ref[...])
    hw = 2 * LANES
    for hd in range(MLA_HEADS):
        nope = q[:, hd * hw:hd * hw + LANES]
        rope = q[:, hd * hw + LANES:(hd + 1) * hw]
        nope = _rms(nope, nn_ref[...])
        ms = jnp.sum(rope * rope, axis=-1, keepdims=True) * (1.0 / MLA_ROPE)
        rope = rope * lax.rsqrt(ms + NORM_EPS) * rn_ref[...]
        rope = _rope(rope, c_ref[...], slo_ref[...], shi_ref[...], MLA_ROPE // 2)
        q_ref[:, hd * hw:hd * hw + LANES] = (nope * scale).astype(BF16)
        q_ref[:, hd * hw + LANES:(hd + 1) * hw] = (rope * scale).astype(BF16)


def _mla_kv_body(x_ref, w_ref, kn_ref, kr_ref, k_ref, v_ref):
    kvb = _dot(x_ref[...], w_ref[...])
    hw = 2 * LANES
    kr = kr_ref[...]
    for hd in range(MLA_HEADS):
        k_ref[:, hd * hw:hd * hw + LANES] = _rms(kvb[:, hd * LANES:(hd + 1) * LANES], kn_ref[...]).astype(BF16)
        k_ref[:, hd * hw + LANES:(hd + 1) * hw] = kr
    v_ref[...] = kvb[:, MLA_HEADS * LANES:].astype(BF16)


def mla_layer(lay, h, mod, gain, w_down, qa_norm, w_qb, kv_norm, w_kvb, qn_nope, qn_rope, kn_nope, kn_rope, w_o):
    d = lay.d
    tabs = rope_tables(lay, MLA_ROPE)
    pad64 = lambda t: jnp.concatenate([t, jnp.zeros((LANES - MLA_ROPE,), F32)]).reshape(1, LANES)
    wd = jnp.pad(w_down, ((0, 0), (0, LANES - MLA_ROPE))).astype(BF16)
    nd = wd.shape[1]
    q_lat, ckv, kr = pl.pallas_call(
        _mla_down_body,
        grid=(lay.nblocks,),
        in_specs=[_rows(d), _full((1, d)), _modspec(lay), _full((d, nd)), _full((1, MLA_Q_RANK)),
                  _full((1, MLA_KV_RANK)), _full((1, LANES))] + [_posrows(lay)] * 3,
        out_specs=[_rows(MLA_Q_RANK), _rows(MLA_KV_RANK), _rows(LANES)],
        out_shape=[jax.ShapeDtypeStruct((lay.n, MLA_Q_RANK), BF16), jax.ShapeDtypeStruct((lay.n, MLA_KV_RANK), BF16),
                   jax.ShapeDtypeStruct((lay.n, LANES), BF16)],
        compiler_params=_cparams(("parallel",)),
        name="mla_down",
    )(h, gain.reshape(1, d), mod, wd, qa_norm.reshape(1, -1), kv_norm.reshape(1, -1), pad64(kn_rope), *tabs)

    hq = MLA_NOPE + MLA_ROPE
    wq = w_qb.reshape(MLA_Q_RANK, MLA_HEADS, hq)
    wq = jnp.pad(wq, ((0, 0), (0, 0), (0, 2 * LANES - hq))).reshape(MLA_Q_RANK, MLA_HEADS * 2 * LANES).astype(BF16)
    dq = 2 * LANES
    scale = float(hq) ** -0.5
    q = pl.pallas_call(
        functools.partial(_mla_q_body, scale),
        grid=(lay.nblocks,),
        in_specs=[_rows(MLA_Q_RANK), _full(wq.shape), _full((1, LANES)), _full((1, LANES))] + [_posrows(lay)] * 3,
        out_specs=_rows(MLA_HEADS * dq),
        out_shape=jax.ShapeDtypeStruct((lay.n, MLA_HEADS * dq), BF16),
        compiler_params=_cparams(("parallel",)),
        name="mla_q",
    )(q_lat, wq, qn_nope.reshape(1, LANES), pad64(qn_rope), *tabs)

    wkv_ = w_kvb.reshape(MLA_KV_RANK, MLA_HEADS, MLA_NOPE + MLA_V)
    wkv_ = jnp.concatenate([wkv_[:, :, :MLA_NOPE].reshape(MLA_KV_RANK, -1),
                            wkv_[:, :, MLA_NOPE:].reshape(MLA_KV_RANK, -1)], axis=1).astype(BF16)
    k, v = pl.pallas_call(
        _mla_kv_body,
        grid=(lay.nblocks,),
        in_specs=[_rows(MLA_KV_RANK), _full(wkv_.shape), _full((1, LANES)), _rows(LANES)],
        out_specs=[_rows(MLA_HEADS * dq), _rows(MLA_HEADS * MLA_V)],
        out_shape=[jax.ShapeDtypeStruct((lay.n, MLA_HEADS * dq), BF16),
                   jax.ShapeDtypeStruct((lay.n, MLA_HEADS * MLA_V), BF16)],
        compiler_params=_cparams(("parallel",)),
        name="mla_kv",
    )(ckv, wkv_, kn_nope.reshape(1, LANES), kr)

    o = attention(lay, q, k, v, MLA_HEADS, 1, dq, MLA_V)
    return out_proj(lay, o, w_o, jnp.zeros((d,), F32), h, mod)


def _dft_cos_sin(n):
    j = np.arange(n, dtype=np.int64)
    m = (j[:, None] * j[None, :]) % n
    ang = 2.0 * np.pi * m.astype(np.float64) / n
    return np.cos(ang), np.sin(ang)


def _fn_stage1_body(gw, h_ref, g_ref, mod_ref, cs_ref, zc_ref, zs_ref):
    u = (_rms(h_ref[...], g_ref[...]) * (1.0 + mod_ref[0, 1:2, :]) + mod_ref[0, 0:1, :]).astype(BF16)
    for gi in range(FN_GROUPS):
        z = jnp.dot(u[:, gi * gw:(gi + 1) * gw], cs_ref[...], preferred_element_type=F32)
        zc_ref[:, gi * gw:(gi + 1) * gw] = z[:, 0:gw].astype(BF16)
        zs_ref[:, gi * gw:(gi + 1) * gw] = z[:, gw:2 * gw].astype(BF16)


def _fn_stage2_body(lay, ct_ref, st_ref, zc_ref, zs_ref, o_ref):
    j = pl.program_id(1)

    def mix(lo, hi):
        f = (jnp.dot(ct_ref[:, lo:hi], zc_ref[lo:hi, :], preferred_element_type=F32)
             - jnp.dot(st_ref[:, lo:hi], zs_ref[lo:hi, :], preferred_element_type=F32))
        o_ref[...] = f.astype(BF16)

    @pl.when(j < lay.cb)
    def _():
        mix(0, lay.ctx)

    @pl.when(j >= lay.cb)
    def _():
        mix(lay.ctx, lay.tt)


def fourier_layer(lay, h, mod, gain, w, b):
    d = lay.d
    gw = d // FN_GROUPS
    cw, sw = _dft_cos_sin(gw)
    csw = jnp.asarray(np.concatenate([cw, sw], axis=1), F32).astype(BF16)
    zc, zs = pl.pallas_call(
        functools.partial(_fn_stage1_body, gw),
        grid=(lay.nblocks,),
        in_specs=[_rows(d), _full((1, d)), _modspec(lay), _full((gw, 2 * gw))],
        out_specs=[_rows(d), _rows(d)],
        out_shape=[jax.ShapeDtypeStruct((lay.n, d), BF16)] * 2,
        compiler_params=_cparams(("parallel",)),
        name="fn_stage1",
    )(h, gain.reshape(1, d), mod, csw)

    ct = np.zeros((lay.tt, lay.tt), np.float64)
    st = np.zeros((lay.tt, lay.tt), np.float64)
    for lo, n in ((0, lay.ctx), (lay.ctx, lay.seq)):
        c, s = _dft_cos_sin(n)
        sc = 1.0 / math.sqrt(n * gw)
        ct[lo:lo + n, lo:lo + n] = c * sc
        st[lo:lo + n, lo:lo + n] = s * sc
    ct = jnp.asarray(ct, F32).astype(BF16)
    st = jnp.asarray(st, F32).astype(BF16)
    zc3 = zc.reshape(lay.b, lay.tt, d)
    zs3 = zs.reshape(lay.b, lay.tt, d)
    f = pl.pallas_call(
        functools.partial(_fn_stage2_body, lay),
        grid=(lay.b, lay.nb),
        in_specs=[pl.BlockSpec((TM, lay.tt), lambda bb, j: (j, 0)),
                  pl.BlockSpec((TM, lay.tt), lambda bb, j: (j, 0)),
                  pl.BlockSpec((None, lay.tt, d), lambda bb, j: (bb, 0, 0)),
                  pl.BlockSpec((None, lay.tt, d), lambda bb, j: (bb, 0, 0))],
        out_specs=pl.BlockSpec((TM, d), lambda bb, j: (bb * lay.nb + j, 0)),
        out_shape=jax.ShapeDtypeStruct((lay.n, d), BF16),
        compiler_params=_cparams(("parallel", "arbitrary")),
        name="fn_stage2",
    )(ct, st, zc3, zs3)
    return out_proj(lay, f, w, b, h, mod)


def _gqa_qkv_body(scale, h_ref, g_ref, mod_ref, w_ref, qn_ref, kn_ref, c_ref, slo_ref, shi_ref,
                  q_ref, k_ref, v_ref):
    u = _rms(h_ref[...], g_ref[...]) * (1.0 + mod_ref[0, 1:2, :]) + mod_ref[0, 0:1, :]
    qkv = _dot(u, w_ref[...])
    c, slo, shi = c_ref[...], slo_ref[...], shi_ref[...]
    n_q = GQA_HEADS * GQA_HEAD
    n_kv = GQA_KV_HEADS * GQA_HEAD
    for hd in range(GQA_HEADS):
        x = _rms(qkv[:, hd * LANES:(hd + 1) * LANES], qn_ref[...])
        q_ref[:, hd * LANES:(hd + 1) * LANES] = (_rope(x, c, slo, shi, GQA_HEAD // 2) * scale).astype(BF16)
    for hd in range(GQA_KV_HEADS):
        x = _rms(qkv[:, n_q + hd * LANES:n_q + (hd + 1) * LANES], kn_ref[...])
        k_ref[:, hd * LANES:(hd + 1) * LANES] = _rope(x, c, slo, shi, GQA_HEAD // 2).astype(BF16)
    v_ref[...] = qkv[:, n_q + n_kv:].astype(BF16)


def gqa_layer(lay, h, mod, gain, w_qkv, q_norm, k_norm, w_o):
    d = lay.d
    tabs = rope_tables(lay, GQA_HEAD)
    n_q = GQA_HEADS * GQA_HEAD
    n_kv = GQA_KV_HEADS * GQA_HEAD
    q, k, v = pl.pallas_call(
        functools.partial(_gqa_qkv_body, float(GQA_HEAD) ** -0.5),
        grid=(lay.nblocks,),
        in_specs=[_rows(d), _full((1, d)), _modspec(lay), _full((d, n_q + 2 * n_kv)), _full((1, LANES)),
                  _full((1, LANES))] + [_posrows(lay)] * 3,
        out_specs=[_rows(n_q), _rows(n_kv), _rows(n_kv)],
        out_shape=[jax.ShapeDtypeStruct((lay.n, n_q), BF16), jax.ShapeDtypeStruct((lay.n, n_kv), BF16),
                   jax.ShapeDtypeStruct((lay.n, n_kv), BF16)],
        compiler_params=_cparams(("parallel",)),
        name="gqa_qkv",
    )(h, gain.reshape(1, d), mod, w_qkv.astype(BF16), q_norm.reshape(1, LANES), k_norm.reshape(1, LANES), *tabs)
    o = attention(lay, q, k, v, GQA_HEADS, GQA_HEADS // GQA_KV_HEADS, GQA_HEAD, GQA_HEAD)
    return out_proj(lay, o, w_o, jnp.zeros((d,), F32), h, mod)


def _row_tiles(d):
    return d // LANES


def _store_row_tiles(ref, x):
    rows, d = x.shape
    nt = _row_tiles(d)
    for j in range(nt):
        ref[pl.ds(j, rows, stride=nt), :] = x[:, j * LANES:(j + 1) * LANES]


def _load_row_tiles(ref, rows, d):
    nt = _row_tiles(d)
    return jnp.concatenate([ref[pl.ds(j, rows, stride=nt), :] for j in range(nt)], axis=-1)


def _ffn_prep_body(h_ref, g_ref, mod_ref, wr_ref, br_ref, tri_ref, v_ref, idx_ref, wt_ref, rank_ref, cnt_ref,
                   carry_ref):
    @pl.when(pl.program_id(0) == 0)
    def _():
        carry_ref[...] = jnp.zeros_like(carry_ref)

    v = _rms(h_ref[...], g_ref[...]) * (1.0 + mod_ref[0, 4:5, :]) + mod_ref[0, 3:4, :]
    _store_row_tiles(v_ref, v)
    logit = _dot(v, wr_ref[...]) + br_ref[...]
    lane = lax.broadcasted_iota(jnp.int32, logit.shape, 1).astype(F32)
    idx = jnp.zeros_like(logit)
    ex = jnp.zeros_like(logit)
    den = 0.0
    m0 = None
    hot = []
    for kx in range(TOP_K):
        m = jnp.max(logit, axis=-1, keepdims=True)
        am = jnp.min(jnp.where(logit == m, lane, float(LANES)), axis=-1, keepdims=True)
        if kx == 0:
            m0 = m
        e = jnp.exp(m - m0)
        den = den + e
        idx = jnp.where(lane == kx, am, idx)
        ex = jnp.where(lane == kx, e, ex)
        hot.append(lane == am)
        logit = jnp.where(hot[kx], -jnp.inf, logit)
    idx_ref[...] = idx.astype(jnp.int32)
    wt_ref[...] = ex / den
    chosen = jnp.where(hot[0] | hot[1] | hot[2] | hot[3], 1.0, 0.0)
    before = carry_ref[...] + jnp.dot(tri_ref[...], chosen.astype(BF16), preferred_element_type=F32)
    rank = jnp.zeros_like(logit)
    for kx in range(TOP_K):
        rk = jnp.sum(jnp.where(hot[kx], before, 0.0), axis=-1, keepdims=True)
        rank = jnp.where(lane == kx, rk, rank)
    rank_ref[...] = rank.astype(jnp.int32)
    carry_ref[...] = carry_ref[...] + jnp.sum(chosen, axis=0, keepdims=True)
    cnt_ref[...] = carry_ref[...]


def ffn_prep(lay, h, gain, mod, w_r, b_r):
    d = lay.d
    ne = w_r.shape[1]
    wr = jnp.pad(w_r, ((0, 0), (0, LANES - ne))).astype(BF16)
    br = jnp.concatenate([b_r, jnp.full((LANES - ne,), -1e30, F32)]).reshape(1, LANES)
    tri = jnp.asarray(np.tril(np.ones((TM, TM), np.float32), -1)).astype(BF16)
    return pl.pallas_call(
        _ffn_prep_body,
        grid=(lay.nblocks,),
        in_specs=[_rows(d), _full((1, d)), _modspec(lay), _full((d, LANES)), _full((1, LANES)), _full((TM, TM))],
        out_specs=[_rows(LANES, TM * _row_tiles(d)), _rows(LANES), _rows(LANES), _rows(LANES), _full((1, LANES))],
        out_shape=[jax.ShapeDtypeStruct((lay.n * _row_tiles(d), LANES), F32),
                   jax.ShapeDtypeStruct((lay.n, LANES), jnp.int32),
                   jax.ShapeDtypeStruct((lay.n, LANES), F32), jax.ShapeDtypeStruct((lay.n, LANES), jnp.int32),
                   jax.ShapeDtypeStruct((1, LANES), F32)],
        scratch_shapes=[pltpu.VMEM((1, LANES), F32)],
        compiler_params=_cparams(("arbitrary",)),
        name="ffn_prep",
    )(h, gain.reshape(1, d), mod, wr, br, tri)


def _row_at(ref, row, nt):
    return ref.at[pl.ds(pl.multiple_of(row * nt, nt), nt)]


def _dispatch_body(nt, ne, pad_ref, dest_ref, v_ref, xs_ref, zero_ref, sem):
    @pl.when(pl.program_id(0) == 0)
    def _():
        zero_ref[...] = jnp.zeros_like(zero_ref)
        for e in range(ne):
            def issue_zero(i, carry, e=e):
                pltpu.make_async_copy(zero_ref, _row_at(xs_ref, pad_ref[0, e] + i, nt), sem).start()
                return carry

            lax.fori_loop(0, pad_ref[1, e], issue_zero, 0)
        for e in range(ne):
            def drain_zero(i, carry):
                pltpu.make_async_copy(zero_ref, _row_at(xs_ref, 0, nt), sem).wait()
                return carry

            lax.fori_loop(0, pad_ref[1, e], drain_zero, 0)

    def issue(t, carry):
        for kx in range(TOP_K):
            pltpu.make_async_copy(_row_at(v_ref, t, nt), _row_at(xs_ref, dest_ref[0, 0, t * TOP_K + kx], nt),
                                  sem).start(priority=kx % 2)
        return carry

    lax.fori_loop(0, TM, issue, 0)

    def drain(t, carry):
        for kx in range(TOP_K):
            pltpu.make_async_copy(_row_at(v_ref, 0, nt), _row_at(xs_ref, 0, nt), sem).wait()
        return carry

    lax.fori_loop(0, TM, drain, 0)


def dispatch(lay, v, dest3, pad_info, n_rows):
    nt = _row_tiles(lay.d)
    ne = pad_info.shape[1]
    grid_spec = pltpu.PrefetchScalarGridSpec(
        num_scalar_prefetch=1,
        grid=(lay.nblocks,),
        in_specs=[pl.BlockSpec((1, 1, TM * TOP_K), lambda i, pad: (i, 0, 0), memory_space=pltpu.SMEM),
                  pl.BlockSpec((TM * nt, LANES), lambda i, pad: (i, 0))],
        out_specs=pl.BlockSpec(memory_space=pl.ANY),
        scratch_shapes=[pltpu.VMEM((nt, LANES), v.dtype), pltpu.SemaphoreType.DMA],
    )
    return pl.pallas_call(
        functools.partial(_dispatch_body, nt, ne),
        grid_spec=grid_spec,
        out_shape=jax.ShapeDtypeStruct((n_rows * nt, LANES), v.dtype),
        compiler_params=_cparams(("arbitrary",)),
        name="moe_dispatch",
    )(pad_info, dest3, v)


def _experts_body(layer, ff, be_ref, bv_ref, first_ref, next_ref, x_ref, bg_ref, bu_ref, bd_ref, wg_hbm, wu_hbm, wd_hbm,
                  y_ref, stage_gu, stage_d, work_gu, work_d, sem):
    i = pl.program_id(0)
    d = x_ref.shape[0] // EXPERT_ROWS * LANES

    def copies(e):
        return (pltpu.make_async_copy(wg_hbm.at[layer, e], stage_gu.at[:, pl.ds(0, ff)], sem),
                pltpu.make_async_copy(wu_hbm.at[layer, e], stage_gu.at[:, pl.ds(ff, ff)], sem),
                pltpu.make_async_copy(wd_hbm.at[layer, e], stage_d, sem))

    @pl.when(i == 0)
    def _():
        for cp in copies(be_ref[0]):
            cp.start()

    @pl.when(first_ref[i] > 0)
    def _():
        for cp in copies(be_ref[i]):
            cp.wait()
        rows = 64

        def cast_gu(c, carry):
            r0 = pl.multiple_of(c * rows, rows)
            work_gu[pl.ds(r0, rows), :] = stage_gu[pl.ds(r0, rows), :].astype(BF16)
            return carry

        lax.fori_loop(0, d // rows, cast_gu, 0)

        def cast_d(c, carry):
            r0 = pl.multiple_of(c * rows, rows)
            work_d[pl.ds(r0, rows), :] = stage_d[pl.ds(r0, rows), :].astype(BF16)
            return carry

        lax.fori_loop(0, ff // rows, cast_d, 0)

        @pl.when(next_ref[i] >= 0)
        def _():
            for cp in copies(next_ref[i]):
                cp.start()

    @pl.when(bv_ref[i] > 0)
    def _():
        x = _load_row_tiles(x_ref, EXPERT_ROWS, d).astype(BF16)
        hgu = jnp.dot(x, work_gu[...], preferred_element_type=F32)
        gt = jnp.minimum(hgu[:, 0:ff] + bg_ref[0], SWIGLU_LIMIT)
        up = jnp.clip(hgu[:, ff:2 * ff] + bu_ref[0], -SWIGLU_LIMIT, SWIGLU_LIMIT)
        act = gt * _sigmoid(SWIGLU_ALPHA * gt) * (up + 1.0)
        _store_row_tiles(y_ref, _dot(act, work_d[...]) + bd_ref[0])

    @pl.when(bv_ref[i] == 0)
    def _():
        y_ref[...] = jnp.zeros_like(y_ref)


def experts(layer, xs, blk_e, blk_valid, blk_first, blk_next, wg, bg, wu, bu, wd, bd):
    nl, ne, d, ff = wg.shape
    nt = _row_tiles(d)
    n_rows = xs.shape[0] // nt
    r = EXPERT_ROWS
    hbm = pl.BlockSpec(memory_space=pl.ANY)
    grid_spec = pltpu.PrefetchScalarGridSpec(
        num_scalar_prefetch=4,
        grid=(n_rows // r,),
        in_specs=[pl.BlockSpec((r * nt, LANES), lambda i, be, bv, bf, bn: (jnp.where(bv[i] > 0, i, 0), 0)),
                  pl.BlockSpec((None, 1, 1, ff), lambda i, be, bv, bf, bn: (layer, be[i], 0, 0)),
                  pl.BlockSpec((None, 1, 1, ff), lambda i, be, bv, bf, bn: (layer, be[i], 0, 0)),
                  pl.BlockSpec((None, 1, 1, d), lambda i, be, bv, bf, bn: (layer, be[i], 0, 0)),
                  hbm, hbm, hbm],
        out_specs=pl.BlockSpec((r * nt, LANES), lambda i, be, bv, bf, bn: (i, 0)),
        scratch_shapes=[pltpu.VMEM((d, 2 * ff), F32), pltpu.VMEM((ff, d), F32),
                        pltpu.VMEM((d, 2 * ff), BF16), pltpu.VMEM((ff, d), BF16), pltpu.SemaphoreType.DMA],
    )
    return pl.pallas_call(
        functools.partial(_experts_body, layer, ff),
        grid_spec=grid_spec,
        out_shape=jax.ShapeDtypeStruct((n_rows * nt, LANES), F32),
        compiler_params=_cparams(("arbitrary",)),
        name="experts",
    )(blk_e, blk_valid, blk_first, blk_next, xs, bg.reshape(nl, ne, 1, ff), bu.reshape(nl, ne, 1, ff),
      bd.reshape(nl, ne, 1, d), wg, wu, wd)


def _combine_body(nt, nblocks, dest_ref, dnext_ref, wt_ref, h_ref, mod_ref, y_ref, out_ref, ybuf, fbuf, sems):
    i = pl.program_id(0)
    slot = i % 2

    def gather(dref, sl):
        def issue(t, carry):
            for kx in range(TOP_K):
                pltpu.make_async_copy(_row_at(y_ref, dref[0, 0, t * TOP_K + kx], nt), _row_at(ybuf.at[sl, kx], t, nt),
                                      sems.at[sl]).start(priority=kx % 2)
            return carry

        lax.fori_loop(0, TM, issue, 0)

    @pl.when(i == 0)
    def _():
        gather(dest_ref, 0)

    @pl.when(i + 1 < nblocks)
    def _():
        gather(dnext_ref, 1 - slot)

    def drain(t, carry):
        for kx in range(TOP_K):
            pltpu.make_async_copy(_row_at(y_ref, 0, nt), _row_at(ybuf.at[slot, kx], 0, nt), sems.at[slot]).wait()
        return carry

    lax.fori_loop(0, TM, drain, 0)

    def weigh(t, carry):
        base = pl.multiple_of(t * nt, nt)
        acc = wt_ref[0, 0, t * TOP_K] * ybuf[slot, 0, pl.ds(base, nt), :]
        for kx in range(1, TOP_K):
            acc = acc + wt_ref[0, 0, t * TOP_K + kx] * ybuf[slot, kx, pl.ds(base, nt), :]
        fbuf[pl.ds(base, nt), :] = acc
        return carry

    lax.fori_loop(0, TM, weigh, 0, unroll=4)
    out_ref[...] = h_ref[...] + mod_ref[0, 5:6, :] * _load_row_tiles(fbuf, TM, h_ref.shape[1])


def combine(lay, h, y, dest3, wts3, mod):
    d = lay.d
    nt = _row_tiles(d)
    nb = lay.nblocks
    smem = lambda f: pl.BlockSpec((1, 1, TM * TOP_K), f, memory_space=pltpu.SMEM)
    return pl.pallas_call(
        functools.partial(_combine_body, nt, nb),
        grid=(nb,),
        in_specs=[smem(lambda i: (i, 0, 0)), smem(lambda i: (jnp.minimum(i + 1, nb - 1), 0, 0)),
                  smem(lambda i: (i, 0, 0)), _rows(d), _modspec(lay), pl.BlockSpec(memory_space=pl.ANY)],
        out_specs=_rows(d),
        out_shape=jax.ShapeDtypeStruct((lay.n, d), F32),
        scratch_shapes=[pltpu.VMEM((2, TOP_K, TM * nt, LANES), F32), pltpu.VMEM((TM * nt, LANES), F32),
                        pltpu.SemaphoreType.DMA((2,))],
        compiler_params=_cparams(("arbitrary",)),
        name="moe_combine",
    )(dest3, dest3, wts3, h, mod, y)


def moe_layer(lay, layer, h, mod, gain, w_r, b_r, wg, bg, wu, bu, wd, bd):
    ne = w_r.shape[1]
    r = EXPERT_ROWS
    v, idx, wts, rank, cnt = ffn_prep(lay, h, gain, mod, w_r, b_r)
    n_slot = lay.n * TOP_K
    counts = cnt[0, :ne].astype(jnp.int32)
    padded = (counts + r - 1) // r * r
    pad_end = jnp.cumsum(padded)
    pad_start = pad_end - padded
    dest = pad_start[idx[:, :TOP_K]] + rank[:, :TOP_K]
    dest3 = dest.reshape(lay.nblocks, 1, TM * TOP_K)
    n_rows = n_slot + ne * r
    blk_start = jnp.arange(n_rows // r, dtype=jnp.int32) * r
    blk_e = jnp.minimum(jnp.sum((blk_start[:, None] >= pad_end[None, :]).astype(jnp.int32), axis=1), ne - 1)
    blk_valid = (blk_start < pad_end[-1]).astype(jnp.int32)
    pad_info = jnp.stack([jnp.concatenate([pad_start + counts, pad_end[-1:]]),
                          jnp.concatenate([padded - counts, n_rows - pad_end[-1:]])]).astype(jnp.int32)
    xs = dispatch(lay, v, dest3, pad_info, n_rows)
    prev_e = jnp.concatenate([jnp.full((1,), -1, jnp.int32), blk_e[:-1]])
    blk_first = ((blk_e != prev_e) & (blk_valid > 0)).astype(jnp.int32)
    eid = jnp.arange(ne, dtype=jnp.int32)
    later = (eid[None, :] > eid[:, None]) & (padded[None, :] > 0)
    nxt = jnp.min(jnp.where(later, eid[None, :], ne), axis=1)
    blk_next = jnp.where(nxt[blk_e] < ne, nxt[blk_e], -1).astype(jnp.int32)
    y = experts(layer, xs, blk_e, blk_valid, blk_first, blk_next, wg, bg, wu, bu, wd, bd)
    wts3 = wts[:, :TOP_K].reshape(lay.nblocks, 1, TM * TOP_K)
    return combine(lay, h, y, dest3, wts3, mod)


def kernel(x, c, ctx, c_ctx, mod_w, mod_b, norm_mix, norm_ffn, router_w, router_b, exp_w_gate, exp_b_gate, exp_w_up, exp_b_up, exp_w_down, exp_b_down, rw_mu, rw_wr, rw_wk, rw_wv, rw_wo, rw_w0, rw_w1, rw_w2, rw_a0, rw_a1, rw_a2, rw_g1, rw_g2, rw_kk, rw_ka, rw_rk, rw_lnx_g, rw_lnx_b, mla_w_down, mla_qa_norm, mla_w_qb, mla_kv_norm, mla_w_kvb, mla_qn_nope, mla_qn_rope, mla_kn_nope, mla_kn_rope, mla_wo, fn_w, fn_b, gqa_w_qkv, gqa_q_norm, gqa_k_norm, gqa_wo):
    bsz, seq, d = x.shape
    ctx_len = ctx.shape[1]
    depth = mod_w.shape[0]
    lay = Layout(bsz, ctx_len, seq, d)
    assert bsz < 16 and d // PAIR * PAIR == d

    cin = jnp.zeros((16, d), F32).at[:bsz].set(c).at[bsz].set(c_ctx)
    mod_all = modulation(cin, mod_w, mod_b).reshape(depth, 16, 6, d)

    h = jnp.concatenate([ctx, x], axis=1).reshape(lay.n, d)
    n_mixers = 4
    for i in range(depth):
        m, j = i % n_mixers, i // n_mixers
        mod = mod_all[i]
        if m == 0:
            h = rwkv_layer(lay, h, mod, norm_mix[i], rw_mu[j], rw_wr[j], rw_wk[j], rw_wv[j], rw_wo[j], rw_w0[j],
                           rw_w1[j], rw_w2[j], rw_a0[j], rw_a1[j], rw_a2[j], rw_g1[j], rw_g2[j], rw_kk[j],
                           rw_ka[j], rw_rk[j], rw_lnx_g[j], rw_lnx_b[j])
        elif m == 1:
            h = mla_layer(lay, h, mod, norm_mix[i], mla_w_down[j], mla_qa_norm[j], mla_w_qb[j], mla_kv_norm[j],
                          mla_w_kvb[j], mla_qn_nope[j], mla_qn_rope[j], mla_kn_nope[j], mla_kn_rope[j], mla_wo[j])
        elif m == 2:
            h = fourier_layer(lay, h, mod, norm_mix[i], fn_w[j], fn_b[j])
        else:
            h = gqa_layer(lay, h, mod, norm_mix[i], gqa_w_qkv[j], gqa_q_norm[j], gqa_k_norm[j], gqa_wo[j])
        h = moe_layer(lay, i, h, mod, norm_ffn[i], router_w[i], router_b[i], exp_w_gate, exp_b_gate, exp_w_up, exp_b_up,
                      exp_w_down, exp_b_down)
    return h.reshape(bsz, lay.tt, d)[:, ctx_len:, :]
```

```python
import functools
import math

import numpy as np
import jax
import jax.numpy as jnp
from jax import lax
from jax.experimental import pallas as pl
from jax.experimental.pallas import tpu as pltpu

F32 = jnp.float32
BF16 = jnp.bfloat16

LANES = 128
SUBLANES = 8
VMEM_LIMIT_BYTES = 56 * 1024 * 1024

GRID_W = 64
ROPE_THETA = 10000.0
NORM_EPS = 1e-6
RW_HEAD = 64
RW_GN_EPS = 64e-5
MLA_HEADS = 16
MLA_Q_RANK = 512
MLA_KV_RANK = 512
MLA_NOPE = 128
MLA_ROPE = 64
MLA_V = 128
FN_GROUPS = 8
GQA_HEADS = 16
GQA_KV_HEADS = 4
GQA_HEAD = 128
TOP_K = 4
SWIGLU_LIMIT = 7.0
SWIGLU_ALPHA = 1.702

TM = 256
WKV_CHUNK = 64
WKV_GROUP = 16
ATTN_ROWS = 256
EXPERT_ROWS = 256
PAIR = 2 * RW_HEAD


def _cparams(sem):
    return pltpu.CompilerParams(dimension_semantics=sem, vmem_limit_bytes=VMEM_LIMIT_BYTES)


def _dot(a, b):
    return jnp.dot(a.astype(BF16), b.astype(BF16), preferred_element_type=F32)


def _dot_nt(a, b):
    return lax.dot_general(a.astype(BF16), b.astype(BF16), (((1,), (1,)), ((), ())),
                           preferred_element_type=F32)


def _rms(x, gain):
    return x * lax.rsqrt(jnp.mean(x * x, axis=-1, keepdims=True) + NORM_EPS) * gain


def _sigmoid(x):
    return 1.0 / (1.0 + jnp.exp(-x))


class Layout:
    def __init__(self, batch, ctx_len, seq, d):
        self.b, self.ctx, self.seq, self.d = batch, ctx_len, seq, d
        self.tt = ctx_len + seq
        self.n = batch * self.tt
        assert ctx_len % TM == 0 and seq % TM == 0
        self.nb = self.tt // TM
        self.cb = ctx_len // TM
        self.nblocks = self.n // TM

    def seg(self, i):
        return jnp.where(i % self.nb < self.cb, self.b, i // self.nb)


def _rows(ncols, tm=TM):
    return pl.BlockSpec((tm, ncols), lambda i: (i, 0))


def _full(shape):
    nd = len(shape)
    return pl.BlockSpec(shape, lambda i: (0,) * nd)


def _modspec(lay):
    return pl.BlockSpec((1, 6, lay.d), lambda i: (lay.seg(i), 0, 0))


def _posrows(lay, ncols=LANES):
    return pl.BlockSpec((TM, ncols), lambda i: (i % lay.nb, 0))


def _pairs(tm=TM):
    return pl.BlockSpec((None, tm, PAIR), lambda i: (0, i, 0))


def _mod_body(c_ref, w_ref, b_ref, o_ref):
    c = c_ref[...]
    s = c * _sigmoid(c)
    o_ref[0] = _dot(s, w_ref[0]) + b_ref[0]


def modulation(cin, mod_w, mod_b):
    depth, d, n6 = mod_w.shape
    tn = 1024
    return pl.pallas_call(
        _mod_body,
        grid=(depth, n6 // tn),
        in_specs=[pl.BlockSpec((16, d), lambda l, j: (0, 0)),
                  pl.BlockSpec((1, d, tn), lambda l, j: (l, 0, j)),
                  pl.BlockSpec((1, 1, tn), lambda l, j: (l, 0, j))],
        out_specs=pl.BlockSpec((1, 16, tn), lambda l, j: (l, 0, j)),
        out_shape=jax.ShapeDtypeStruct((depth, 16, n6), F32),
        compiler_params=_cparams(("parallel", "parallel")),
        name="modulation",
    )(cin, mod_w, mod_b.reshape(depth, 1, n6))


def _rw_prep_body(lay, h_ref, hp_ref, hn_ref, g_ref, mod_ref, mu_ref, *outs):
    j = pl.program_id(0) % lay.nb
    seg_start = jnp.logical_or(j == 0, j == lay.cb)
    seg_end = jnp.logical_or(j == lay.cb - 1, j == lay.nb - 1)
    gain = g_ref[...]
    shift, scale = mod_ref[0, 0:1, :], mod_ref[0, 1:2, :]

    def norm_mod(x):
        return _rms(x, gain) * (1.0 + scale) + shift

    u = norm_mod(h_ref[...])
    prev = jnp.where(seg_start, 0.0, norm_mod(hp_ref[SUBLANES - 1:SUBLANES, :]))
    nxt = jnp.where(seg_end, 0.0, norm_mod(hn_ref[0:1, :]))
    row = lax.broadcasted_iota(jnp.int32, u.shape, 0)
    up = jnp.where(row == 0, prev, pltpu.roll(u, 1, 0))
    un = jnp.where(row == TM - 1, nxt, pltpu.roll(u, TM - 1, 0))
    du = 0.5 * (up + un) - u
    for n, o_ref in enumerate(outs):
        o_ref[...] = (u + du * mu_ref[n:n + 1, :]).astype(BF16)


def rw_prep(lay, h, gain, mod, mu):
    d = lay.d
    r8 = TM // SUBLANES
    last8 = lay.n // SUBLANES - 1
    mu8 = jnp.concatenate([mu, jnp.zeros((2, d), F32)], axis=0)
    return pl.pallas_call(
        functools.partial(_rw_prep_body, lay),
        grid=(lay.nblocks,),
        in_specs=[_rows(d),
                  pl.BlockSpec((SUBLANES, d), lambda i: (jnp.maximum(i * r8 - 1, 0), 0)),
                  pl.BlockSpec((SUBLANES, d), lambda i: (jnp.minimum((i + 1) * r8, last8), 0)),
                  _full((1, d)), _modspec(lay), _full((8, d))],
        out_specs=[_rows(d)] * 6,
        out_shape=[jax.ShapeDtypeStruct((lay.n, d), BF16)] * 6,
        compiler_params=_cparams(("parallel",)),
        name="rw_prep",
    )(h, h, h, gain.reshape(1, d), mod, mu8)


def _linear_pairs_body(npairs, x_ref, w_ref, o_ref):
    y = _dot(x_ref[...], w_ref[...])
    for p in range(npairs):
        o_ref[p] = y[:, p * PAIR:(p + 1) * PAIR]


def linear_pairs(lay, x, w):
    d = lay.d
    npairs = d // PAIR
    return pl.pallas_call(
        functools.partial(_linear_pairs_body, npairs),
        grid=(lay.nblocks,),
        in_specs=[_rows(d), _full((d, d))],
        out_specs=pl.BlockSpec((npairs, TM, PAIR), lambda i: (0, i, 0)),
        out_shape=jax.ShapeDtypeStruct((npairs, lay.n, PAIR), F32),
        compiler_params=_cparams(("parallel",)),
        name="rw_linear",
    )(x, w)


def _rw_lora_body(npairs, xw_ref, xa_ref, xg_ref, w1_ref, w2_ref, a1_ref, a2_ref, g1_ref, g2_ref,
                  w0_ref, a0_ref, lw_ref, ag_ref, g_ref):
    hw = jnp.tanh(_dot(xw_ref[...], w1_ref[...]))
    ha = _dot(xa_ref[...], a1_ref[...])
    hg = _sigmoid(_dot(xg_ref[...], g1_ref[...]))
    rw = w1_ref.shape[1] // 2
    rg = g1_ref.shape[1] // 2
    for dr in range(2):
        z = w0_ref[dr:dr + 1, :] + _dot(hw[:, dr * rw:(dr + 1) * rw], w2_ref[dr])
        lw = -math.exp(-0.5) * _sigmoid(z)
        a = _sigmoid(a0_ref[dr:dr + 1, :] + _dot(ha[:, dr * rw:(dr + 1) * rw], a2_ref[dr]))
        g = _dot(hg[:, dr * rg:(dr + 1) * rg], g2_ref[dr])
        for p in range(npairs):
            sl = slice(p * PAIR, (p + 1) * PAIR)
            lw_ref[dr, p] = lw[:, sl]
            ag_ref[dr, p] = a[:, sl]
            g_ref[dr, p] = g[:, sl]


def _pad_lora(w1, w2):
    r = w1.shape[2]
    rp = -(-r // LANES) * LANES
    w1p = jnp.pad(w1, ((0, 0), (0, 0), (0, rp - r)))
    w1p = jnp.concatenate([w1p[0], w1p[1]], axis=1).astype(BF16)
    w2p = jnp.pad(w2, ((0, 0), (0, rp - r), (0, 0))).astype(BF16)
    return w1p, w2p


def rw_lora(lay, xw, xa, xg, w0, w1, w2, a0, a1, a2, g1, g2):
    d = lay.d
    npairs = d // PAIR
    w1p, w2p = _pad_lora(w1, w2)
    a1p, a2p = _pad_lora(a1, a2)
    g1p, g2p = _pad_lora(g1, g2)
    ospec = pl.BlockSpec((2, npairs, TM, PAIR), lambda i: (0, 0, i, 0))
    oshape = jax.ShapeDtypeStruct((2, npairs, lay.n, PAIR), F32)
    return pl.pallas_call(
        functools.partial(_rw_lora_body, npairs),
        grid=(lay.nblocks,),
        in_specs=[_rows(d)] * 3 + [_full(w1p.shape), _full(w2p.shape), _full(a1p.shape), _full(a2p.shape),
                                    _full(g1p.shape), _full(g2p.shape), _full((2, d)), _full((2, d))],
        out_specs=[ospec] * 3,
        out_shape=[oshape] * 3,
        compiler_params=_cparams(("parallel",)),
        name="rw_lora",
    )(xw, xa, xg, w1p, w2p, a1p, a2p, g1p, g2p, w0, a0)


def _bd(z, head0):
    return jnp.concatenate([jnp.where(head0, z, 0.0), jnp.where(head0, 0.0, z)], axis=0)


def _head_sum(x, head0):
    s0 = jnp.sum(jnp.where(head0, x, 0.0), axis=-1, keepdims=True)
    s1 = jnp.sum(jnp.where(head0, 0.0, x), axis=-1, keepdims=True)
    return jnp.where(head0, s0, s1)


def _wkv_group_chunk(rev, tiles, par, hts):
    c = WKV_CHUNK
    rng = range(len(tiles))
    t = lax.broadcasted_iota(jnp.int32, (c, PAIR), 0)
    lane = lax.broadcasted_iota(jnp.int32, (c, PAIR), 1)
    s = lane % RW_HEAD
    head0 = lane < RW_HEAD
    strict = (s > t) if rev else (s < t)
    incl = (s >= t) if rev else (s <= t)
    eye = jnp.where(s == t, 1.0, 0.0)
    rr = lax.broadcasted_iota(jnp.int32, (PAIR, PAIR), 0) // RW_HEAD
    cc = lax.broadcasted_iota(jnp.int32, (PAIR, PAIR), 1) // RW_HEAD
    same_head = rr == cc

    def prep(q):
        r, k, v, lw, ag, _ = tiles[q]
        kk_p, ka_p = par[q][0], par[q][1]
        kkf = k * kk_p
        kk = kkf / jnp.maximum(jnp.sqrt(_head_sum(kkf * kkf, head0)), 1e-12)
        kd = k * (1.0 + (ag - 1.0) * ka_p)
        bvec = kk * ag
        cs = lw
        for sh in (1, 2, 4, 8, 16, 32):
            if rev:
                cs = cs + jnp.where(t + sh < c, pltpu.roll(cs, c - sh, 0), 0.0)
            else:
                cs = cs + jnp.where(t >= sh, pltpu.roll(cs, sh, 0), 0.0)
        cl = cs[0:1, :] if rev else cs[c - 1:c, :]
        e_neg = jnp.exp(-cs)
        e_end = jnp.exp(cl - cs)
        ar = jnp.concatenate([-kk * jnp.exp(cs - lw), r * jnp.exp(cs)], axis=0).astype(BF16)
        bk = jnp.concatenate([_bd(bvec * e_neg, head0), _bd(kd * e_neg, head0)], axis=0).astype(BF16)
        bk_end = jnp.concatenate([bvec * e_end, kd * e_end], axis=0).astype(BF16)
        return dict(ar=ar, bk=bk, bk_end=bk_end, kd=kd, decay=jnp.exp(cl), vbd=_bd(v, head0).astype(BF16))

    st = [prep(q) for q in rng]
    p = [_dot_nt(st[q]["ar"], st[q]["bk"]) for q in rng]
    arh = [_dot_nt(st[q]["ar"], hts[q]) for q in rng]
    l_ab = [jnp.where(strict, p[q][0:c, 0:PAIR], 0.0) for q in rng]
    l_ak = [jnp.where(strict, p[q][0:c, PAIR:2 * PAIR], 0.0) for q in rng]
    m_r = [jnp.concatenate([jnp.where(incl, p[q][c:2 * c, 0:PAIR], 0.0),
                            jnp.where(incl, p[q][c:2 * c, PAIR:2 * PAIR], 0.0)], axis=1).astype(BF16) for q in rng]
    x = [arh[q][0:c] + _dot(l_ak[q], st[q]["vbd"]) for q in rng]
    tinv = [eye + l_ab[q] for q in rng]
    pw = [_dot(l_ab[q], _bd(l_ab[q], head0)) for q in rng]
    for _ in range(4):
        both = [_dot(jnp.concatenate([pw[q], tinv[q]], axis=0), _bd(pw[q], head0)) for q in rng]
        tinv = [tinv[q] + both[q][c:2 * c] for q in rng]
        pw = [both[q][0:c] for q in rng]
    tinv = [tinv[q] + _dot(tinv[q], _bd(pw[q], head0)) for q in rng]
    u = [_dot(tinv[q], _bd(x[q], head0)) for q in rng]
    y = [arh[q][c:2 * c] + _dot(m_r[q], jnp.concatenate([_bd(u[q], head0).astype(BF16), st[q]["vbd"]], axis=0))
         for q in rng]
    upd = [_dot(jnp.concatenate([u[q], tiles[q][2]], axis=0).T, st[q]["bk_end"]) for q in rng]
    ht_new = [hts[q] * st[q]["decay"] + jnp.where(same_head, upd[q], 0.0) for q in rng]

    outs = []
    inv_n = 1.0 / RW_HEAD
    for q in rng:
        r, _, v, _, _, g = tiles[q]
        rk_p, lg_p, lb_p = par[q][2], par[q][3], par[q][4]
        mu = _head_sum(y[q], head0) * inv_n
        yc = y[q] - mu
        var = _head_sum(yc * yc, head0) * inv_n
        bonus = _head_sum(r * st[q]["kd"] * rk_p, head0) * v
        outs.append((yc * lax.rsqrt(var + RW_GN_EPS) * lg_p + lb_p + bonus) * g)
    return outs, ht_new


def _wkv_body(rev, npairs, r_ref, k_ref, v_ref, lw_ref, ag_ref, g_ref, kk_ref, ka_ref, rk_ref, lg_ref, lb_ref,
              o_ref, ht_ref):
    @pl.when(pl.program_id(1) == 0)
    def _():
        ht_ref[...] = jnp.zeros_like(ht_ref)

    def group(gi, carry):
        ps = [gi * WKV_GROUP + q for q in range(WKV_GROUP)]
        tiles = [(r_ref[p], k_ref[p], v_ref[p], lw_ref[0, p], ag_ref[0, p], g_ref[0, p]) for p in ps]
        par = [(kk_ref[p], ka_ref[p], rk_ref[p], lg_ref[p], lb_ref[p]) for p in ps]
        outs, hts = _wkv_group_chunk(rev, tiles, par, [ht_ref[p] for p in ps])
        for q, p in enumerate(ps):
            o_ref[p] = outs[q]
            ht_ref[p] = hts[q]
        return carry

    lax.fori_loop(0, npairs // WKV_GROUP, group, 0)


def wkv(lay, rev, r, k, v, lw, ag, g, params):
    npairs = lay.d // PAIR
    c = WKV_CHUNK
    nch = lay.tt // c
    cch = lay.ctx // c
    dr = 1 if rev else 0

    def rowblk(b, ci):
        if rev:
            return b * nch + jnp.where(ci < cch, cch - 1 - ci, nch + cch - 1 - ci)
        return b * nch + ci

    tok = pl.BlockSpec((npairs, c, PAIR), lambda b, ci: (0, rowblk(b, ci), 0))
    tokd = pl.BlockSpec((1, npairs, c, PAIR), lambda b, ci: (dr, 0, rowblk(b, ci), 0))
    par = pl.BlockSpec((npairs, 1, PAIR), lambda b, ci: (0, 0, 0))
    return pl.pallas_call(
        functools.partial(_wkv_body, rev, npairs),
        grid=(lay.b, nch),
        in_specs=[tok, tok, tok, tokd, tokd, tokd] + [par] * 5,
        out_specs=tok,
        out_shape=jax.ShapeDtypeStruct((npairs, lay.n, PAIR), F32),
        scratch_shapes=[pltpu.VMEM((npairs, PAIR, PAIR), F32)],
        compiler_params=_cparams(("parallel", "arbitrary")),
        name="wkv_rev" if rev else "wkv_fwd",
    )(r, k, v, lw, ag, g, *params)


def _rw_out_body(npairs, o0_ref, o1_ref, h_ref, mod_ref, w_ref, out_ref):
    acc = jnp.concatenate([o0_ref[p] + o1_ref[p] for p in range(npairs)], axis=-1)
    out_ref[...] = h_ref[...] + mod_ref[0, 2:3, :] * _dot(acc, w_ref[...])


def rw_out(lay, o0, o1, h, mod, w):
    d = lay.d
    npairs = d // PAIR
    pm = pl.BlockSpec((npairs, TM, PAIR), lambda i: (0, i, 0))
    return pl.pallas_call(
        functools.partial(_rw_out_body, npairs),
        grid=(lay.nblocks,),
        in_specs=[pm, pm, _rows(d), _modspec(lay), _full((d, d))],
        out_specs=_rows(d),
        out_shape=jax.ShapeDtypeStruct((lay.n, d), F32),
        compiler_params=_cparams(("parallel",)),
        name="rw_out",
    )(o0, o1, h, mod, w)


def rwkv_layer(lay, h, mod, gain, mu, w_r, w_k, w_v, w_o, w0, w1, w2, a0, a1, a2, g1, g2, k_k, k_a, r_k,
               lnx_g, lnx_b):
    npairs = lay.d // PAIR
    xr, xw, xk, xv, xa, xg = rw_prep(lay, h, gain, mod, mu)
    r = linear_pairs(lay, xr, w_r.astype(BF16))
    k = linear_pairs(lay, xk, w_k.astype(BF16))
    v = linear_pairs(lay, xv, w_v.astype(BF16))
    lw, ag, g = rw_lora(lay, xw, xa, xg, w0, w1, w2, a0, a1, a2, g1, g2)
    params = [t.reshape(npairs, 1, PAIR) for t in (k_k, k_a, r_k.reshape(-1), lnx_g, lnx_b)]
    o0 = wkv(lay, False, r, k, v, lw, ag, g, params)
    o1 = wkv(lay, True, r, k, v, lw, ag, g, params)
    return rw_out(lay, o0, o1, h, mod, w_o.astype(BF16))


def rope_tables(lay, rot_dim):
    quarter = rot_dim // 4
    half = rot_dim // 2
    t = np.arange(lay.seq)
    inv = ROPE_THETA ** (-np.arange(quarter, dtype=np.float32) / quarter)
    ang = np.concatenate([(t // GRID_W)[:, None].astype(np.float32) * inv,
                          (t % GRID_W)[:, None].astype(np.float32) * inv], axis=-1)
    ang = jnp.asarray(ang, F32)
    cos, sin = jnp.cos(ang), jnp.sin(ang)
    pad = LANES - rot_dim
    zer = jnp.zeros((lay.seq, half), F32)
    c = jnp.concatenate([cos, cos, jnp.ones((lay.seq, pad), F32)], axis=-1)
    s_lo = jnp.concatenate([-sin, zer, jnp.zeros((lay.seq, pad), F32)], axis=-1)
    s_hi = jnp.concatenate([zer, sin, jnp.zeros((lay.seq, pad), F32)], axis=-1)

    def with_ctx(tab, fill):
        return jnp.concatenate([jnp.full((lay.ctx, LANES), fill, F32), tab], axis=0)

    return with_ctx(c, 1.0), with_ctx(s_lo, 0.0), with_ctx(s_hi, 0.0)


def _rope(x, c, s_lo, s_hi, half):
    return x * c + pltpu.roll(x, LANES - half, 1) * s_lo + pltpu.roll(x, half, 1) * s_hi


def _attn_rows(ctx, rc, row0, q_ref, k, v, o_ref):
    n = q_ref.shape[0] // rc
    ss = []
    for c in range(n):
        s = _dot_nt(q_ref[c * rc:(c + 1) * rc, :], k)
        if row0 is not None and row0 + c * rc < ctx:
            qrow = row0 + c * rc + lax.broadcasted_iota(jnp.int32, s.shape, 0)
            kcol = lax.broadcasted_iota(jnp.int32, s.shape, 1)
            s = jnp.where(jnp.logical_or(qrow >= ctx, kcol < ctx), s, -1e30)
        ss.append(s)
    ms = [jnp.max(s, axis=-1, keepdims=True) for s in ss]
    ps = [jnp.exp(ss[c] - ms[c]).astype(BF16) for c in range(n)]
    dv = v.shape[1]
    v1 = jnp.concatenate([v, jnp.ones_like(v)], axis=1)
    for c in range(n):
        o = jnp.dot(ps[c], v1, preferred_element_type=F32)
        o_ref[c * rc:(c + 1) * rc, :] = (o[:, 0:dv] / o[:, dv:dv + 1]).astype(o_ref.dtype)


def _attn_body(ctx, q_ref, k_ref, v_ref, o_ref):
    k = k_ref[...]
    v = v_ref[...]

    @pl.when(pl.program_id(2) == 0)
    def _():
        _attn_rows(ctx, ATTN_ROWS, 0, q_ref, k, v, o_ref)

    @pl.when(pl.program_id(2) != 0)
    def _():
        _attn_rows(ctx, ATTN_ROWS, None, q_ref, k, v, o_ref)


def attention(lay, q, k, v, n_heads, kv_group, dq, dv, tq=768):
    nq = lay.tt // tq
    assert lay.ctx <= tq and tq % ATTN_ROWS == 0
    k3 = k.reshape(lay.b, lay.tt, k.shape[1])
    v3 = v.reshape(lay.b, lay.tt, v.shape[1])
    return pl.pallas_call(
        functools.partial(_attn_body, lay.ctx),
        grid=(lay.b, n_heads, nq),
        in_specs=[pl.BlockSpec((tq, dq), lambda b, h, i: (b * nq + i, h)),
                  pl.BlockSpec((None, lay.tt, dq), lambda b, h, i: (b, 0, h // kv_group)),
                  pl.BlockSpec((None, lay.tt, dv), lambda b, h, i: (b, 0, h // kv_group))],
        out_specs=pl.BlockSpec((tq, dv), lambda b, h, i: (b * nq + i, h)),
        out_shape=jax.ShapeDtypeStruct((lay.n, n_heads * dv), BF16),
        compiler_params=_cparams(("parallel", "parallel", "arbitrary")),
        name="attention",
    )(q, k3, v3)


def _out_proj_body(x_ref, w_ref, b_ref, h_ref, mod_ref, out_ref):
    y = _dot(x_ref[...], w_ref[...]) + b_ref[...]
    out_ref[...] = h_ref[...] + mod_ref[0, 2:3, :] * y


def out_proj(lay, x, w, bias, h, mod):
    d = lay.d
    kdim = x.shape[1]
    return pl.pallas_call(
        _out_proj_body,
        grid=(lay.nblocks,),
        in_specs=[_rows(kdim), _full((kdim, d)), _full((1, d)), _rows(d), _modspec(lay)],
        out_specs=_rows(d),
        out_shape=jax.ShapeDtypeStruct((lay.n, d), F32),
        compiler_params=_cparams(("parallel",)),
        name="out_proj",
    )(x, w.astype(BF16), bias.reshape(1, d), h, mod)


def _mla_down_body(h_ref, g_ref, mod_ref, w_ref, qan_ref, kvn_ref, krn_ref, c_ref, slo_ref, shi_ref,
                   q_ref, ckv_ref, kr_ref):
    u = _rms(h_ref[...], g_ref[...]) * (1.0 + mod_ref[0, 1:2, :]) + mod_ref[0, 0:1, :]
    down = _dot(u, w_ref[...])
    q_ref[...] = _rms(down[:, 0:MLA_Q_RANK], qan_ref[...]).astype(BF16)
    ckv_ref[...] = _rms(down[:, MLA_Q_RANK:MLA_Q_RANK + MLA_KV_RANK], kvn_ref[...]).astype(BF16)
    kr = down[:, MLA_Q_RANK + MLA_KV_RANK:]
    ms = jnp.sum(kr * kr, axis=-1, keepdims=True) * (1.0 / MLA_ROPE)
    kr = kr * lax.rsqrt(ms + NORM_EPS) * krn_ref[...]
    kr_ref[...] = _rope(kr, c_ref[...], slo_ref[...], shi_ref[...], MLA_ROPE // 2).astype(BF16)


def _mla_q_body(scale, x_ref, w_ref, nn_ref, rn_ref, c_ref, slo_ref, shi_ref, q_ref):
    q = _dot(x_ref[...], w_ref[...])
    hw = 2 * LANES
    for hd in range(MLA_HEADS):
        nope = q[:, hd * hw:hd * hw + LANES]
        rope = q[:, hd * hw + LANES:(hd + 1) * hw]
        nope = _rms(nope, nn_ref[...])
        ms = jnp.sum(rope * rope, axis=-1, keepdims=True) * (1.0 / MLA_ROPE)
        rope = rope * lax.rsqrt(ms + NORM_EPS) * rn_ref[...]
        rope = _rope(rope, c_ref[...], slo_ref[...], shi_ref[...], MLA_ROPE // 2)
        q_ref[:, hd * hw:hd * hw + LANES] = (nope * scale).astype(BF16)
        q_ref[:, hd * hw + LANES:(hd + 1) * hw] = (rope * scale).astype(BF16)


def _mla_kv_body(x_ref, w_ref, kn_ref, kr_ref, k_ref, v_ref):
    kvb = _dot(x_ref[...], w_ref[...])
    hw = 2 * LANES
    kr = kr_ref[...]
    for hd in range(MLA_HEADS):
        k_ref[:, hd * hw:hd * hw + LANES] = _rms(kvb[:, hd * LANES:(hd + 1) * LANES], kn_ref[...]).astype(BF16)
        k_ref[:, hd * hw + LANES:(hd + 1) * hw] = kr
    v_ref[...] = kvb[:, MLA_HEADS * LANES:].astype(BF16)


def mla_layer(lay, h, mod, gain, w_down, qa_norm, w_qb, kv_norm, w_kvb, qn_nope, qn_rope, kn_nope, kn_rope, w_o):
    d = lay.d
    tabs = rope_tables(lay, MLA_ROPE)
    pad64 = lambda t: jnp.concatenate([t, jnp.zeros((LANES - MLA_ROPE,), F32)]).reshape(1, LANES)
    wd = jnp.pad(w_down, ((0, 0), (0, LANES - MLA_ROPE))).astype(BF16)
    nd = wd.shape[1]
    q_lat, ckv, kr = pl.pallas_call(
        _mla_down_body,
        grid=(lay.nblocks,),
        in_specs=[_rows(d), _full((1, d)), _modspec(lay), _full((d, nd)), _full((1, MLA_Q_RANK)),
                  _full((1, MLA_KV_RANK)), _full((1, LANES))] + [_posrows(lay)] * 3,
        out_specs=[_rows(MLA_Q_RANK), _rows(MLA_KV_RANK), _rows(LANES)],
        out_shape=[jax.ShapeDtypeStruct((lay.n, MLA_Q_RANK), BF16), jax.ShapeDtypeStruct((lay.n, MLA_KV_RANK), BF16),
                   jax.ShapeDtypeStruct((lay.n, LANES), BF16)],
        compiler_params=_cparams(("parallel",)),
        name="mla_down",
    )(h, gain.reshape(1, d), mod, wd, qa_norm.reshape(1, -1), kv_norm.reshape(1, -1), pad64(kn_rope), *tabs)

    hq = MLA_NOPE + MLA_ROPE
    wq = w_qb.reshape(MLA_Q_RANK, MLA_HEADS, hq)
    wq = jnp.pad(wq, ((0, 0), (0, 0), (0, 2 * LANES - hq))).reshape(MLA_Q_RANK, MLA_HEADS * 2 * LANES).astype(BF16)
    dq = 2 * LANES
    scale = float(hq) ** -0.5
    q = pl.pallas_call(
        functools.partial(_mla_q_body, scale),
        grid=(lay.nblocks,),
        in_specs=[_rows(MLA_Q_RANK), _full(wq.shape), _full((1, LANES)), _full((1, LANES))] + [_posrows(lay)] * 3,
        out_specs=_rows(MLA_HEADS * dq),
        out_shape=jax.ShapeDtypeStruct((lay.n, MLA_HEADS * dq), BF16),
        compiler_params=_cparams(("parallel",)),
        name="mla_q",
    )(q_lat, wq, qn_nope.reshape(1, LANES), pad64(qn_rope), *tabs)

    wkv_ = w_kvb.reshape(MLA_KV_RANK, MLA_HEADS, MLA_NOPE + MLA_V)
    wkv_ = jnp.concatenate([wkv_[:, :, :MLA_NOPE].reshape(MLA_KV_RANK, -1),
                            wkv_[:, :, MLA_NOPE:].reshape(MLA_KV_RANK, -1)], axis=1).astype(BF16)
    k, v = pl.pallas_call(
        _mla_kv_body,
        grid=(lay.nblocks,),
        in_specs=[_rows(MLA_KV_RANK), _full(wkv_.shape), _full((1, LANES)), _rows(LANES)],
        out_specs=[_rows(MLA_HEADS * dq), _rows(MLA_HEADS * MLA_V)],
        out_shape=[jax.ShapeDtypeStruct((lay.n, MLA_HEADS * dq), BF16),
                   jax.ShapeDtypeStruct((lay.n, MLA_HEADS * MLA_V), BF16)],
        compiler_params=_cparams(("parallel",)),
        name="mla_kv",
    )(ckv, wkv_, kn_nope.reshape(1, LANES), kr)

    o = attention(lay, q, k, v, MLA_HEADS, 1, dq, MLA_V)
    return out_proj(lay, o, w_o, jnp.zeros((d,), F32), h, mod)


def _dft_cos_sin(n):
    j = np.arange(n, dtype=np.int64)
    m = (j[:, None] * j[None, :]) % n
    ang = 2.0 * np.pi * m.astype(np.float64) / n
    return np.cos(ang), np.sin(ang)


def _fn_stage1_body(gw, h_ref, g_ref, mod_ref, cs_ref, zc_ref, zs_ref):
    u = (_rms(h_ref[...], g_ref[...]) * (1.0 + mod_ref[0, 1:2, :]) + mod_ref[0, 0:1, :]).astype(BF16)
    for gi in range(FN_GROUPS):
        z = jnp.dot(u[:, gi * gw:(gi + 1) * gw], cs_ref[...], preferred_element_type=F32)
        zc_ref[:, gi * gw:(gi + 1) * gw] = z[:, 0:gw].astype(BF16)
        zs_ref[:, gi * gw:(gi + 1) * gw] = z[:, gw:2 * gw].astype(BF16)


def _fn_stage2_body(lay, ct_ref, st_ref, zc_ref, zs_ref, o_ref):
    j = pl.program_id(1)

    def mix(lo, hi):
        f = (jnp.dot(ct_ref[:, lo:hi], zc_ref[lo:hi, :], preferred_element_type=F32)
             - jnp.dot(st_ref[:, lo:hi], zs_ref[lo:hi, :], preferred_element_type=F32))
        o_ref[...] = f.astype(BF16)

    @pl.when(j < lay.cb)
    def _():
        mix(0, lay.ctx)

    @pl.when(j >= lay.cb)
    def _():
        mix(lay.ctx, lay.tt)


def fourier_layer(lay, h, mod, gain, w, b):
    d = lay.d
    gw = d // FN_GROUPS
    cw, sw = _dft_cos_sin(gw)
    csw = jnp.asarray(np.concatenate([cw, sw], axis=1), F32).astype(BF16)
    zc, zs = pl.pallas_call(
        functools.partial(_fn_stage1_body, gw),
        grid=(lay.nblocks,),
        in_specs=[_rows(d), _full((1, d)), _modspec(lay), _full((gw, 2 * gw))],
        out_specs=[_rows(d), _rows(d)],
        out_shape=[jax.ShapeDtypeStruct((lay.n, d), BF16)] * 2,
        compiler_params=_cparams(("parallel",)),
        name="fn_stage1",
    )(h, gain.reshape(1, d), mod, csw)

    ct = np.zeros((lay.tt, lay.tt), np.float64)
    st = np.zeros((lay.tt, lay.tt), np.float64)
    for lo, n in ((0, lay.ctx), (lay.ctx, lay.seq)):
        c, s = _dft_cos_sin(n)
        sc = 1.0 / math.sqrt(n * gw)
        ct[lo:lo + n, lo:lo + n] = c * sc
        st[lo:lo + n, lo:lo + n] = s * sc
    ct = jnp.asarray(ct, F32).astype(BF16)
    st = jnp.asarray(st, F32).astype(BF16)
    zc3 = zc.reshape(lay.b, lay.tt, d)
    zs3 = zs.reshape(lay.b, lay.tt, d)
    f = pl.pallas_call(
        functools.partial(_fn_stage2_body, lay),
        grid=(lay.b, lay.nb),
        in_specs=[pl.BlockSpec((TM, lay.tt), lambda bb, j: (j, 0)),
                  pl.BlockSpec((TM, lay.tt), lambda bb, j: (j, 0)),
                  pl.BlockSpec((None, lay.tt, d), lambda bb, j: (bb, 0, 0)),
                  pl.BlockSpec((None, lay.tt, d), lambda bb, j: (bb, 0, 0))],
        out_specs=pl.BlockSpec((TM, d), lambda bb, j: (bb * lay.nb + j, 0)),
        out_shape=jax.ShapeDtypeStruct((lay.n, d), BF16),
        compiler_params=_cparams(("parallel", "arbitrary")),
        name="fn_stage2",
    )(ct, st, zc3, zs3)
    return out_proj(lay, f, w, b, h, mod)


def _gqa_qkv_body(scale, h_ref, g_ref, mod_ref, w_ref, qn_ref, kn_ref, c_ref, slo_ref, shi_ref,
                  q_ref, k_ref, v_ref):
    u = _rms(h_ref[...], g_ref[...]) * (1.0 + mod_ref[0, 1:2, :]) + mod_ref[0, 0:1, :]
    qkv = _dot(u, w_ref[...])
    c, slo, shi = c_ref[...], slo_ref[...], shi_ref[...]
    n_q = GQA_HEADS * GQA_HEAD
    n_kv = GQA_KV_HEADS * GQA_HEAD
    for hd in range(GQA_HEADS):
        x = _rms(qkv[:, hd * LANES:(hd + 1) * LANES], qn_ref[...])
        q_ref[:, hd * LANES:(hd + 1) * LANES] = (_rope(x, c, slo, shi, GQA_HEAD // 2) * scale).astype(BF16)
    for hd in range(GQA_KV_HEADS):
        x = _rms(qkv[:, n_q + hd * LANES:n_q + (hd + 1) * LANES], kn_ref[...])
        k_ref[:, hd * LANES:(hd + 1) * LANES] = _rope(x, c, slo, shi, GQA_HEAD // 2).astype(BF16)
    v_ref[...] = qkv[:, n_q + n_kv:].astype(BF16)


def gqa_layer(lay, h, mod, gain, w_qkv, q_norm, k_norm, w_o):
    d = lay.d
    tabs = rope_tables(lay, GQA_HEAD)
    n_q = GQA_HEADS * GQA_HEAD
    n_kv = GQA_KV_HEADS * GQA_HEAD
    q, k, v = pl.pallas_call(
        functools.partial(_gqa_qkv_body, float(GQA_HEAD) ** -0.5),
        grid=(lay.nblocks,),
        in_specs=[_rows(d), _full((1, d)), _modspec(lay), _full((d, n_q + 2 * n_kv)), _full((1, LANES)),
                  _full((1, LANES))] + [_posrows(lay)] * 3,
        out_specs=[_rows(n_q), _rows(n_kv), _rows(n_kv)],
        out_shape=[jax.ShapeDtypeStruct((lay.n, n_q), BF16), jax.ShapeDtypeStruct((lay.n, n_kv), BF16),
                   jax.ShapeDtypeStruct((lay.n, n_kv), BF16)],
        compiler_params=_cparams(("parallel",)),
        name="gqa_qkv",
    )(h, gain.reshape(1, d), mod, w_qkv.astype(BF16), q_norm.reshape(1, LANES), k_norm.reshape(1, LANES), *tabs)
    o = attention(lay, q, k, v, GQA_HEADS, GQA_HEADS // GQA_KV_HEADS, GQA_HEAD, GQA_HEAD)
    return out_proj(lay, o, w_o, jnp.zeros((d,), F32), h, mod)


def _row_tiles(d):
    return d // LANES


def _store_row_tiles(ref, x):
    rows, d = x.shape
    nt = _row_tiles(d)
    for j in range(nt):
        ref[pl.ds(j, rows, stride=nt), :] = x[:, j * LANES:(j + 1) * LANES]


def _load_row_tiles(ref, rows, d):
    nt = _row_tiles(d)
    return jnp.concatenate([ref[pl.ds(j, rows, stride=nt), :] for j in range(nt)], axis=-1)


def _ffn_prep_body(h_ref, g_ref, mod_ref, wr_ref, br_ref, tri_ref, v_ref, idx_ref, wt_ref, rank_ref, cnt_ref,
                   carry_ref):
    @pl.when(pl.program_id(0) == 0)
    def _():
        carry_ref[...] = jnp.zeros_like(carry_ref)

    v = _rms(h_ref[...], g_ref[...]) * (1.0 + mod_ref[0, 4:5, :]) + mod_ref[0, 3:4, :]
    _store_row_tiles(v_ref, v)
    logit = _dot(v, wr_ref[...]) + br_ref[...]
    lane = lax.broadcasted_iota(jnp.int32, logit.shape, 1).astype(F32)
    idx = jnp.zeros_like(logit)
    ex = jnp.zeros_like(logit)
    den = 0.0
    m0 = None
    hot = []
    for kx in range(TOP_K):
        m = jnp.max(logit, axis=-1, keepdims=True)
        am = jnp.min(jnp.where(logit == m, lane, float(LANES)), axis=-1, keepdims=True)
        if kx == 0:
            m0 = m
        e = jnp.exp(m - m0)
        den = den + e
        idx = jnp.where(lane == kx, am, idx)
        ex = jnp.where(lane == kx, e, ex)
        hot.append(lane == am)
        logit = jnp.where(hot[kx], -jnp.inf, logit)
    idx_ref[...] = idx.astype(jnp.int32)
    wt_ref[...] = ex / den
    chosen = jnp.where(hot[0] | hot[1] | hot[2] | hot[3], 1.0, 0.0)
    before = carry_ref[...] + jnp.dot(tri_ref[...], chosen.astype(BF16), preferred_element_type=F32)
    rank = jnp.zeros_like(logit)
    for kx in range(TOP_K):
        rk = jnp.sum(jnp.where(hot[kx], before, 0.0), axis=-1, keepdims=True)
        rank = jnp.where(lane == kx, rk, rank)
    rank_ref[...] = rank.astype(jnp.int32)
    carry_ref[...] = carry_ref[...] + jnp.sum(chosen, axis=0, keepdims=True)
    cnt_ref[...] = carry_ref[...]


def ffn_prep(lay, h, gain, mod, w_r, b_r):
    d = lay.d
    ne = w_r.shape[1]
    wr = jnp.pad(w_r, ((0, 0), (0, LANES - ne))).astype(BF16)
    br = jnp.concatenate([b_r, jnp.full((LANES - ne,), -1e30, F32)]).reshape(1, LANES)
    tri = jnp.asarray(np.tril(np.ones((TM, TM), np.float32), -1)).astype(BF16)
    return pl.pallas_call(
        _ffn_prep_body,
        grid=(lay.nblocks,),
        in_specs=[_rows(d), _full((1, d)), _modspec(lay), _full((d, LANES)), _full((1, LANES)), _full((TM, TM))],
        out_specs=[_rows(LANES, TM * _row_tiles(d)), _rows(LANES), _rows(LANES), _rows(LANES), _full((1, LANES))],
        out_shape=[jax.ShapeDtypeStruct((lay.n * _row_tiles(d), LANES), F32),
                   jax.ShapeDtypeStruct((lay.n, LANES), jnp.int32),
                   jax.ShapeDtypeStruct((lay.n, LANES), F32), jax.ShapeDtypeStruct((lay.n, LANES), jnp.int32),
                   jax.ShapeDtypeStruct((1, LANES), F32)],
        scratch_shapes=[pltpu.VMEM((1, LANES), F32)],
        compiler_params=_cparams(("arbitrary",)),
        name="ffn_prep",
    )(h, gain.reshape(1, d), mod, wr, br, tri)


def _row_at(ref, row, nt):
    return ref.at[pl.ds(pl.multiple_of(row * nt, nt), nt)]


def _dispatch_body(nt, ne, pad_ref, dest_ref, v_ref, xs_ref, zero_ref, sem):
    @pl.when(pl.program_id(0) == 0)
    def _():
        zero_ref[...] = jnp.zeros_like(zero_ref)
        for e in range(ne):
            def issue_zero(i, carry, e=e):
                pltpu.make_async_copy(zero_ref, _row_at(xs_ref, pad_ref[0, e] + i, nt), sem).start()
                return carry

            lax.fori_loop(0, pad_ref[1, e], issue_zero, 0)
        for e in range(ne):
            def drain_zero(i, carry):
                pltpu.make_async_copy(zero_ref, _row_at(xs_ref, 0, nt), sem).wait()
                return carry

            lax.fori_loop(0, pad_ref[1, e], drain_zero, 0)

    def issue(t, carry):
        for kx in range(TOP_K):
            pltpu.make_async_copy(_row_at(v_ref, t, nt), _row_at(xs_ref, dest_ref[0, 0, t * TOP_K + kx], nt),
                                  sem).start()
        return carry

    lax.fori_loop(0, TM, issue, 0)

    def drain(t, carry):
        for kx in range(TOP_K):
            pltpu.make_async_copy(_row_at(v_ref, 0, nt), _row_at(xs_ref, 0, nt), sem).wait()
        return carry

    lax.fori_loop(0, TM, drain, 0)


def dispatch(lay, v, dest3, pad_info, n_rows):
    nt = _row_tiles(lay.d)
    ne = pad_info.shape[1]
    grid_spec = pltpu.PrefetchScalarGridSpec(
        num_scalar_prefetch=1,
        grid=(lay.nblocks,),
        in_specs=[pl.BlockSpec((1, 1, TM * TOP_K), lambda i, pad: (i, 0, 0), memory_space=pltpu.SMEM),
                  pl.BlockSpec((TM * nt, LANES), lambda i, pad: (i, 0))],
        out_specs=pl.BlockSpec(memory_space=pl.ANY),
        scratch_shapes=[pltpu.VMEM((nt, LANES), v.dtype), pltpu.SemaphoreType.DMA],
    )
    return pl.pallas_call(
        functools.partial(_dispatch_body, nt, ne),
        grid_spec=grid_spec,
        out_shape=jax.ShapeDtypeStruct((n_rows * nt, LANES), v.dtype),
        compiler_params=_cparams(("arbitrary",)),
        name="moe_dispatch",
    )(pad_info, dest3, v)


def _experts_body(layer, ff, be_ref, bv_ref, first_ref, next_ref, x_ref, bg_ref, bu_ref, bd_ref, wg_hbm, wu_hbm, wd_hbm,
                  y_ref, stage_gu, stage_d, work_gu, work_d, sem):
    i = pl.program_id(0)
    d = x_ref.shape[0] // EXPERT_ROWS * LANES

    def copies(e):
        return (pltpu.make_async_copy(wg_hbm.at[layer, e], stage_gu.at[:, pl.ds(0, ff)], sem),
                pltpu.make_async_copy(wu_hbm.at[layer, e], stage_gu.at[:, pl.ds(ff, ff)], sem),
                pltpu.make_async_copy(wd_hbm.at[layer, e], stage_d, sem))

    @pl.when(i == 0)
    def _():
        for cp in copies(be_ref[0]):
            cp.start()

    @pl.when(first_ref[i] > 0)
    def _():
        for cp in copies(be_ref[i]):
            cp.wait()
        rows = 64

        def cast_gu(c, carry):
            r0 = pl.multiple_of(c * rows, rows)
            work_gu[pl.ds(r0, rows), :] = stage_gu[pl.ds(r0, rows), :].astype(BF16)
            return carry

        lax.fori_loop(0, d // rows, cast_gu, 0)

        def cast_d(c, carry):
            r0 = pl.multiple_of(c * rows, rows)
            work_d[pl.ds(r0, rows), :] = stage_d[pl.ds(r0, rows), :].astype(BF16)
            return carry

        lax.fori_loop(0, ff // rows, cast_d, 0)

        @pl.when(next_ref[i] >= 0)
        def _():
            for cp in copies(next_ref[i]):
                cp.start()

    @pl.when(bv_ref[i] > 0)
    def _():
        x = _load_row_tiles(x_ref, EXPERT_ROWS, d).astype(BF16)
        hgu = jnp.dot(x, work_gu[...], preferred_element_type=F32)
        gt = jnp.minimum(hgu[:, 0:ff] + bg_ref[0], SWIGLU_LIMIT)
        up = jnp.clip(hgu[:, ff:2 * ff] + bu_ref[0], -SWIGLU_LIMIT, SWIGLU_LIMIT)
        act = gt * _sigmoid(SWIGLU_ALPHA * gt) * (up + 1.0)
        _store_row_tiles(y_ref, _dot(act, work_d[...]) + bd_ref[0])

    @pl.when(bv_ref[i] == 0)
    def _():
        y_ref[...] = jnp.zeros_like(y_ref)


def experts(layer, xs, blk_e, blk_valid, blk_first, blk_next, wg, bg, wu, bu, wd, bd):
    nl, ne, d, ff = wg.shape
    nt = _row_tiles(d)
    n_rows = xs.shape[0] // nt
    r = EXPERT_ROWS
    hbm = pl.BlockSpec(memory_space=pl.ANY)
    grid_spec = pltpu.PrefetchScalarGridSpec(
        num_scalar_prefetch=4,
        grid=(n_rows // r,),
        in_specs=[pl.BlockSpec((r * nt, LANES), lambda i, be, bv, bf, bn: (jnp.where(bv[i] > 0, i, 0), 0)),
                  pl.BlockSpec((None, 1, 1, ff), lambda i, be, bv, bf, bn: (layer, be[i], 0, 0)),
                  pl.BlockSpec((None, 1, 1, ff), lambda i, be, bv, bf, bn: (layer, be[i], 0, 0)),
                  pl.BlockSpec((None, 1, 1, d), lambda i, be, bv, bf, bn: (layer, be[i], 0, 0)),
                  hbm, hbm, hbm],
        out_specs=pl.BlockSpec((r * nt, LANES), lambda i, be, bv, bf, bn: (i, 0)),
        scratch_shapes=[pltpu.VMEM((d, 2 * ff), F32), pltpu.VMEM((ff, d), F32),
                        pltpu.VMEM((d, 2 * ff), BF16), pltpu.VMEM((ff, d), BF16), pltpu.SemaphoreType.DMA],
    )
    return pl.pallas_call(
        functools.partial(_experts_body, layer, ff),
        grid_spec=grid_spec,
        out_shape=jax.ShapeDtypeStruct((n_rows * nt, LANES), F32),
        compiler_params=_cparams(("arbitrary",)),
        name="experts",
    )(blk_e, blk_valid, blk_first, blk_next, xs, bg.reshape(nl, ne, 1, ff), bu.reshape(nl, ne, 1, ff),
      bd.reshape(nl, ne, 1, d), wg, wu, wd)


def _combine_body(nt, nblocks, dest_ref, dnext_ref, wt_ref, h_ref, mod_ref, y_ref, out_ref, ybuf, fbuf, sems):
    i = pl.program_id(0)
    slot = i % 2

    def gather(dref, sl):
        def issue(t, carry):
            for kx in range(TOP_K):
                pltpu.make_async_copy(_row_at(y_ref, dref[0, 0, t * TOP_K + kx], nt), _row_at(ybuf.at[sl, kx], t, nt),
                                      sems.at[sl]).start(priority=kx % 2)
            return carry

        lax.fori_loop(0, TM, issue, 0)

    @pl.when(i == 0)
    def _():
        gather(dest_ref, 0)

    @pl.when(i + 1 < nblocks)
    def _():
        gather(dnext_ref, 1 - slot)

    def drain(t, carry):
        for kx in range(TOP_K):
            pltpu.make_async_copy(_row_at(y_ref, 0, nt), _row_at(ybuf.at[slot, kx], 0, nt), sems.at[slot]).wait()
        return carry

    lax.fori_loop(0, TM, drain, 0)

    def weigh(t, carry):
        base = pl.multiple_of(t * nt, nt)
        acc = wt_ref[0, 0, t * TOP_K] * ybuf[slot, 0, pl.ds(base, nt), :]
        for kx in range(1, TOP_K):
            acc = acc + wt_ref[0, 0, t * TOP_K + kx] * ybuf[slot, kx, pl.ds(base, nt), :]
        fbuf[pl.ds(base, nt), :] = acc
        return carry

    lax.fori_loop(0, TM, weigh, 0, unroll=4)
    out_ref[...] = h_ref[...] + mod_ref[0, 5:6, :] * _load_row_tiles(fbuf, TM, h_ref.shape[1])


def combine(lay, h, y, dest3, wts3, mod):
    d = lay.d
    nt = _row_tiles(d)
    nb = lay.nblocks
    smem = lambda f: pl.BlockSpec((1, 1, TM * TOP_K), f, memory_space=pltpu.SMEM)
    return pl.pallas_call(
        functools.partial(_combine_body, nt, nb),
        grid=(nb,),
        in_specs=[smem(lambda i: (i, 0, 0)), smem(lambda i: (jnp.minimum(i + 1, nb - 1), 0, 0)),
                  smem(lambda i: (i, 0, 0)), _rows(d), _modspec(lay), pl.BlockSpec(memory_space=pl.ANY)],
        out_specs=_rows(d),
        out_shape=jax.ShapeDtypeStruct((lay.n, d), F32),
        scratch_shapes=[pltpu.VMEM((2, TOP_K, TM * nt, LANES), F32), pltpu.VMEM((TM * nt, LANES), F32),
                        pltpu.SemaphoreType.DMA((2,))],
        compiler_params=_cparams(("arbitrary",)),
        name="moe_combine",
    )(dest3, dest3, wts3, h, mod, y)


def moe_layer(lay, layer, h, mod, gain, w_r, b_r, wg, bg, wu, bu, wd, bd):
    ne = w_r.shape[1]
    r = EXPERT_ROWS
    v, idx, wts, rank, cnt = ffn_prep(lay, h, gain, mod, w_r, b_r)
    n_slot = lay.n * TOP_K
    counts = cnt[0, :ne].astype(jnp.int32)
    padded = (counts + r - 1) // r * r
    pad_end = jnp.cumsum(padded)
    pad_start = pad_end - padded
    dest = pad_start[idx[:, :TOP_K]] + rank[:, :TOP_K]
    dest3 = dest.reshape(lay.nblocks, 1, TM * TOP_K)
    n_rows = n_slot + ne * r
    blk_start = jnp.arange(n_rows // r, dtype=jnp.int32) * r
    blk_e = jnp.minimum(jnp.sum((blk_start[:, None] >= pad_end[None, :]).astype(jnp.int32), axis=1), ne - 1)
    blk_valid = (blk_start < pad_end[-1]).astype(jnp.int32)
    pad_info = jnp.stack([jnp.concatenate([pad_start + counts, pad_end[-1:]]),
                          jnp.concatenate([padded - counts, n_rows - pad_end[-1:]])]).astype(jnp.int32)
    xs = dispatch(lay, v, dest3, pad_info, n_rows)
    prev_e = jnp.concatenate([jnp.full((1,), -1, jnp.int32), blk_e[:-1]])
    blk_first = ((blk_e != prev_e) & (blk_valid > 0)).astype(jnp.int32)
    eid = jnp.arange(ne, dtype=jnp.int32)
    later = (eid[None, :] > eid[:, None]) & (padded[None, :] > 0)
    nxt = jnp.min(jnp.where(later, eid[None, :], ne), axis=1)
    blk_next = jnp.where(nxt[blk_e] < ne, nxt[blk_e], -1).astype(jnp.int32)
    y = experts(layer, xs, blk_e, blk_valid, blk_first, blk_next, wg, bg, wu, bu, wd, bd)
    wts3 = wts[:, :TOP_K].reshape(lay.nblocks, 1, TM * TOP_K)
    return combine(lay, h, y, dest3, wts3, mod)


def kernel(x, c, ctx, c_ctx, mod_w, mod_b, norm_mix, norm_ffn, router_w, router_b, exp_w_gate, exp_b_gate, exp_w_up, exp_b_up, exp_w_down, exp_b_down, rw_mu, rw_wr, rw_wk, rw_wv, rw_wo, rw_w0, rw_w1, rw_w2, rw_a0, rw_a1, rw_a2, rw_g1, rw_g2, rw_kk, rw_ka, rw_rk, rw_lnx_g, rw_lnx_b, mla_w_down, mla_qa_norm, mla_w_qb, mla_kv_norm, mla_w_kvb, mla_qn_nope, mla_qn_rope, mla_kn_nope, mla_kn_rope, mla_wo, fn_w, fn_b, gqa_w_qkv, gqa_q_norm, gqa_k_norm, gqa_wo):
    bsz, seq, d = x.shape
    ctx_len = ctx.shape[1]
    depth = mod_w.shape[0]
    lay = Layout(bsz, ctx_len, seq, d)
    assert bsz < 16 and d // PAIR * PAIR == d

    cin = jnp.zeros((16, d), F32).at[:bsz].set(c).at[bsz].set(c_ctx)
    mod_all = modulation(cin, mod_w, mod_b).reshape(depth, 16, 6, d)

    h = jnp.concatenate([ctx, x], axis=1).reshape(lay.n, d)
    n_mixers = 4
    for i in range(depth):
        m, j = i % n_mixers, i // n_mixers
        mod = mod_all[i]
        if m == 0:
            h = rwkv_layer(lay, h, mod, norm_mix[i], rw_mu[j], rw_wr[j], rw_wk[j], rw_wv[j], rw_wo[j], rw_w0[j],
                           rw_w1[j], rw_w2[j], rw_a0[j], rw_a1[j], rw_a2[j], rw_g1[j], rw_g2[j], rw_kk[j],
                           rw_ka[j], rw_rk[j], rw_lnx_g[j], rw_lnx_b[j])
        elif m == 1:
            h = mla_layer(lay, h, mod, norm_mix[i], mla_w_down[j], mla_qa_norm[j], mla_w_qb[j], mla_kv_norm[j],
                          mla_w_kvb[j], mla_qn_nope[j], mla_qn_rope[j], mla_kn_nope[j], mla_kn_rope[j], mla_wo[j])
        elif m == 2:
            h = fourier_layer(lay, h, mod, norm_mix[i], fn_w[j], fn_b[j])
        else:
            h = gqa_layer(lay, h, mod, norm_mix[i], gqa_w_qkv[j], gqa_q_norm[j], gqa_k_norm[j], gqa_wo[j])
        h = moe_layer(lay, i, h, mod, norm_ffn[i], router_w[i], router_b[i], exp_w_gate, exp_b_gate, exp_w_up, exp_b_up,
                      exp_w_down, exp_b_down)
    return h.reshape(bsz, lay.tt, d)[:, ctx_len:, :]
```

```python
import functools
import math

import numpy as np
import jax
import jax.numpy as jnp
from jax import lax
from jax.experimental import pallas as pl
from jax.experimental.pallas import tpu as pltpu

F32 = jnp.float32
BF16 = jnp.bfloat16

LANES = 128
SUBLANES = 8
VMEM_LIMIT_BYTES = 56 * 1024 * 1024

GRID_W = 64
ROPE_THETA = 10000.0
NORM_EPS = 1e-6
RW_HEAD = 64
RW_GN_EPS = 64e-5
MLA_HEADS = 16
MLA_Q_RANK = 512
MLA_KV_RANK = 512
MLA_NOPE = 128
MLA_ROPE = 64
MLA_V = 128
FN_GROUPS = 8
GQA_HEADS = 16
GQA_KV_HEADS = 4
GQA_HEAD = 128
TOP_K = 4
SWIGLU_LIMIT = 7.0
SWIGLU_ALPHA = 1.702

TM = 256
WKV_CHUNK = 64
WKV_GROUP = 16
ATTN_ROWS = 256
EXPERT_ROWS = 256
PAIR = 2 * RW_HEAD


def _cparams(sem):
    return pltpu.CompilerParams(dimension_semantics=sem, vmem_limit_bytes=VMEM_LIMIT_BYTES)


def _dot(a, b):
    return jnp.dot(a.astype(BF16), b.astype(BF16), preferred_element_type=F32)


def _dot_nt(a, b):
    return lax.dot_general(a.astype(BF16), b.astype(BF16), (((1,), (1,)), ((), ())),
                           preferred_element_type=F32)


def _rms(x, gain):
    return x * lax.rsqrt(jnp.mean(x * x, axis=-1, keepdims=True) + NORM_EPS) * gain


def _sigmoid(x):
    return 1.0 / (1.0 + jnp.exp(-x))


class Layout:
    def __init__(self, batch, ctx_len, seq, d):
        self.b, self.ctx, self.seq, self.d = batch, ctx_len, seq, d
        self.tt = ctx_len + seq
        self.n = batch * self.tt
        assert ctx_len % TM == 0 and seq % TM == 0
        self.nb = self.tt // TM
        self.cb = ctx_len // TM
        self.nblocks = self.n // TM

    def seg(self, i):
        return jnp.where(i % self.nb < self.cb, self.b, i // self.nb)


def _rows(ncols, tm=TM):
    return pl.BlockSpec((tm, ncols), lambda i: (i, 0))


def _full(shape):
    nd = len(shape)
    return pl.BlockSpec(shape, lambda i: (0,) * nd)


def _modspec(lay):
    return pl.BlockSpec((1, 6, lay.d), lambda i: (lay.seg(i), 0, 0))


def _posrows(lay, ncols=LANES):
    return pl.BlockSpec((TM, ncols), lambda i: (i % lay.nb, 0))


def _pairs(tm=TM):
    return pl.BlockSpec((None, tm, PAIR), lambda i: (0, i, 0))


def _mod_body(c_ref, w_ref, b_ref, o_ref):
    c = c_ref[...]
    s = c * _sigmoid(c)
    o_ref[0] = _dot(s, w_ref[0]) + b_ref[0]


def modulation(cin, mod_w, mod_b):
    depth, d, n6 = mod_w.shape
    tn = 1024
    return pl.pallas_call(
        _mod_body,
        grid=(depth, n6 // tn),
        in_specs=[pl.BlockSpec((16, d), lambda l, j: (0, 0)),
                  pl.BlockSpec((1, d, tn), lambda l, j: (l, 0, j)),
                  pl.BlockSpec((1, 1, tn), lambda l, j: (l, 0, j))],
        out_specs=pl.BlockSpec((1, 16, tn), lambda l, j: (l, 0, j)),
        out_shape=jax.ShapeDtypeStruct((depth, 16, n6), F32),
        compiler_params=_cparams(("parallel", "parallel")),
        name="modulation",
    )(cin, mod_w, mod_b.reshape(depth, 1, n6))


def _rw_prep_body(lay, h_ref, hp_ref, hn_ref, g_ref, mod_ref, mu_ref, *outs):
    j = pl.program_id(0) % lay.nb
    seg_start = jnp.logical_or(j == 0, j == lay.cb)
    seg_end = jnp.logical_or(j == lay.cb - 1, j == lay.nb - 1)
    gain = g_ref[...]
    shift, scale = mod_ref[0, 0:1, :], mod_ref[0, 1:2, :]

    def norm_mod(x):
        return _rms(x, gain) * (1.0 + scale) + shift

    u = norm_mod(h_ref[...])
    prev = jnp.where(seg_start, 0.0, norm_mod(hp_ref[SUBLANES - 1:SUBLANES, :]))
    nxt = jnp.where(seg_end, 0.0, norm_mod(hn_ref[0:1, :]))
    row = lax.broadcasted_iota(jnp.int32, u.shape, 0)
    up = jnp.where(row == 0, prev, pltpu.roll(u, 1, 0))
    un = jnp.where(row == TM - 1, nxt, pltpu.roll(u, TM - 1, 0))
    du = 0.5 * (up + un) - u
    for n, o_ref in enumerate(outs):
        o_ref[...] = (u + du * mu_ref[n:n + 1, :]).astype(BF16)


def rw_prep(lay, h, gain, mod, mu):
    d = lay.d
    r8 = TM // SUBLANES
    last8 = lay.n // SUBLANES - 1
    mu8 = jnp.concatenate([mu, jnp.zeros((2, d), F32)], axis=0)
    return pl.pallas_call(
        functools.partial(_rw_prep_body, lay),
        grid=(lay.nblocks,),
        in_specs=[_rows(d),
                  pl.BlockSpec((SUBLANES, d), lambda i: (jnp.maximum(i * r8 - 1, 0), 0)),
                  pl.BlockSpec((SUBLANES, d), lambda i: (jnp.minimum((i + 1) * r8, last8), 0)),
                  _full((1, d)), _modspec(lay), _full((8, d))],
        out_specs=[_rows(d)] * 6,
        out_shape=[jax.ShapeDtypeStruct((lay.n, d), BF16)] * 6,
        compiler_params=_cparams(("parallel",)),
        name="rw_prep",
    )(h, h, h, gain.reshape(1, d), mod, mu8)


def _linear_pairs_body(npairs, x_ref, w_ref, o_ref):
    y = _dot(x_ref[...], w_ref[...])
    for p in range(npairs):
        o_ref[p] = y[:, p * PAIR:(p + 1) * PAIR]


def linear_pairs(lay, x, w):
    d = lay.d
    npairs = d // PAIR
    return pl.pallas_call(
        functools.partial(_linear_pairs_body, npairs),
        grid=(lay.nblocks,),
        in_specs=[_rows(d), _full((d, d))],
        out_specs=pl.BlockSpec((npairs, TM, PAIR), lambda i: (0, i, 0)),
        out_shape=jax.ShapeDtypeStruct((npairs, lay.n, PAIR), F32),
        compiler_params=_cparams(("parallel",)),
        name="rw_linear",
    )(x, w)


def _rw_lora_body(npairs, xw_ref, xa_ref, xg_ref, w1_ref, w2_ref, a1_ref, a2_ref, g1_ref, g2_ref,
                  w0_ref, a0_ref, lw_ref, ag_ref, g_ref):
    hw = jnp.tanh(_dot(xw_ref[...], w1_ref[...]))
    ha = _dot(xa_ref[...], a1_ref[...])
    hg = _sigmoid(_dot(xg_ref[...], g1_ref[...]))
    rw = w1_ref.shape[1] // 2
    rg = g1_ref.shape[1] // 2
    for dr in range(2):
        z = w0_ref[dr:dr + 1, :] + _dot(hw[:, dr * rw:(dr + 1) * rw], w2_ref[dr])
        lw = -math.exp(-0.5) * _sigmoid(z)
        a = _sigmoid(a0_ref[dr:dr + 1, :] + _dot(ha[:, dr * rw:(dr + 1) * rw], a2_ref[dr]))
        g = _dot(hg[:, dr * rg:(dr + 1) * rg], g2_ref[dr])
        for p in range(npairs):
            sl = slice(p * PAIR, (p + 1) * PAIR)
            lw_ref[dr, p] = lw[:, sl]
            ag_ref[dr, p] = a[:, sl]
            g_ref[dr, p] = g[:, sl]


def _pad_lora(w1, w2):
    r = w1.shape[2]
    rp = -(-r // LANES) * LANES
    w1p = jnp.pad(w1, ((0, 0), (0, 0), (0, rp - r)))
    w1p = jnp.concatenate([w1p[0], w1p[1]], axis=1).astype(BF16)
    w2p = jnp.pad(w2, ((0, 0), (0, rp - r), (0, 0))).astype(BF16)
    return w1p, w2p


def rw_lora(lay, xw, xa, xg, w0, w1, w2, a0, a1, a2, g1, g2):
    d = lay.d
    npairs = d // PAIR
    w1p, w2p = _pad_lora(w1, w2)
    a1p, a2p = _pad_lora(a1, a2)
    g1p, g2p = _pad_lora(g1, g2)
    ospec = pl.BlockSpec((2, npairs, TM, PAIR), lambda i: (0, 0, i, 0))
    oshape = jax.ShapeDtypeStruct((2, npairs, lay.n, PAIR), F32)
    return pl.pallas_call(
        functools.partial(_rw_lora_body, npairs),
        grid=(lay.nblocks,),
        in_specs=[_rows(d)] * 3 + [_full(w1p.shape), _full(w2p.shape), _full(a1p.shape), _full(a2p.shape),
                                    _full(g1p.shape), _full(g2p.shape), _full((2, d)), _full((2, d))],
        out_specs=[ospec] * 3,
        out_shape=[oshape] * 3,
        compiler_params=_cparams(("parallel",)),
        name="rw_lora",
    )(xw, xa, xg, w1p, w2p, a1p, a2p, g1p, g2p, w0, a0)


def _bd(z, head0):
    return jnp.concatenate([jnp.where(head0, z, 0.0), jnp.where(head0, 0.0, z)], axis=0)


def _head_sum(x, head0):
    s0 = jnp.sum(jnp.where(head0, x, 0.0), axis=-1, keepdims=True)
    s1 = jnp.sum(jnp.where(head0, 0.0, x), axis=-1, keepdims=True)
    return jnp.where(head0, s0, s1)


def _wkv_group_chunk(rev, tiles, par, hts):
    c = WKV_CHUNK
    rng = range(len(tiles))
    t = lax.broadcasted_iota(jnp.int32, (c, PAIR), 0)
    lane = lax.broadcasted_iota(jnp.int32, (c, PAIR), 1)
    s = lane % RW_HEAD
    head0 = lane < RW_HEAD
    strict = (s > t) if rev else (s < t)
    incl = (s >= t) if rev else (s <= t)
    eye = jnp.where(s == t, 1.0, 0.0)
    rr = lax.broadcasted_iota(jnp.int32, (PAIR, PAIR), 0) // RW_HEAD
    cc = lax.broadcasted_iota(jnp.int32, (PAIR, PAIR), 1) // RW_HEAD
    same_head = rr == cc

    def prep(q):
        r, k, v, lw, ag, _ = tiles[q]
        kk_p, ka_p = par[q][0], par[q][1]
        kkf = k * kk_p
        kk = kkf / jnp.maximum(jnp.sqrt(_head_sum(kkf * kkf, head0)), 1e-12)
        kd = k * (1.0 + (ag - 1.0) * ka_p)
        bvec = kk * ag
        cs = lw
        for sh in (1, 2, 4, 8, 16, 32):
            if rev:
                cs = cs + jnp.where(t + sh < c, pltpu.roll(cs, c - sh, 0), 0.0)
            else:
                cs = cs + jnp.where(t >= sh, pltpu.roll(cs, sh, 0), 0.0)
        cl = cs[0:1, :] if rev else cs[c - 1:c, :]
        e_neg = jnp.exp(-cs)
        e_end = jnp.exp(cl - cs)
        ar = jnp.concatenate([-kk * jnp.exp(cs - lw), r * jnp.exp(cs)], axis=0).astype(BF16)
        bk = jnp.concatenate([_bd(bvec * e_neg, head0), _bd(kd * e_neg, head0)], axis=0).astype(BF16)
        bk_end = jnp.concatenate([bvec * e_end, kd * e_end], axis=0).astype(BF16)
        return dict(ar=ar, bk=bk, bk_end=bk_end, kd=kd, decay=jnp.exp(cl), vbd=_bd(v, head0).astype(BF16))

    st = [prep(q) for q in rng]
    p = [_dot_nt(st[q]["ar"], st[q]["bk"]) for q in rng]
    arh = [_dot_nt(st[q]["ar"], hts[q]) for q in rng]
    l_ab = [jnp.where(strict, p[q][0:c, 0:PAIR], 0.0) for q in rng]
    l_ak = [jnp.where(strict, p[q][0:c, PAIR:2 * PAIR], 0.0) for q in rng]
    m_r = [jnp.concatenate([jnp.where(incl, p[q][c:2 * c, 0:PAIR], 0.0),
                            jnp.where(incl, p[q][c:2 * c, PAIR:2 * PAIR], 0.0)], axis=1).astype(BF16) for q in rng]
    x = [arh[q][0:c] + _dot(l_ak[q], st[q]["vbd"]) for q in rng]
    tinv = [eye + l_ab[q] for q in rng]
    pw = [_dot(l_ab[q], _bd(l_ab[q], head0)) for q in rng]
    for _ in range(4):
        both = [_dot(jnp.concatenate([pw[q], tinv[q]], axis=0), _bd(pw[q], head0)) for q in rng]
        tinv = [tinv[q] + both[q][c:2 * c] for q in rng]
        pw = [both[q][0:c] for q in rng]
    tinv = [tinv[q] + _dot(tinv[q], _bd(pw[q], head0)) for q in rng]
    u = [_dot(tinv[q], _bd(x[q], head0)) for q in rng]
    y = [arh[q][c:2 * c] + _dot(m_r[q], jnp.concatenate([_bd(u[q], head0).astype(BF16), st[q]["vbd"]], axis=0))
         for q in rng]
    upd = [_dot(jnp.concatenate([u[q], tiles[q][2]], axis=0).T, st[q]["bk_end"]) for q in rng]
    ht_new = [hts[q] * st[q]["decay"] + jnp.where(same_head, upd[q], 0.0) for q in rng]

    outs = []
    inv_n = 1.0 / RW_HEAD
    for q in rng:
        r, _, v, _, _, g = tiles[q]
        rk_p, lg_p, lb_p = par[q][2], par[q][3], par[q][4]
        mu = _head_sum(y[q], head0) * inv_n
        yc = y[q] - mu
        var = _head_sum(yc * yc, head0) * inv_n
        bonus = _head_sum(r * st[q]["kd"] * rk_p, head0) * v
        outs.append((yc * lax.rsqrt(var + RW_GN_EPS) * lg_p + lb_p + bonus) * g)
    return outs, ht_new


def _wkv_body(rev, npairs, r_ref, k_ref, v_ref, lw_ref, ag_ref, g_ref, kk_ref, ka_ref, rk_ref, lg_ref, lb_ref,
              o_ref, ht_ref):
    @pl.when(pl.program_id(1) == 0)
    def _():
        ht_ref[...] = jnp.zeros_like(ht_ref)

    def group(gi, carry):
        ps = [gi * WKV_GROUP + q for q in range(WKV_GROUP)]
        tiles = [(r_ref[p], k_ref[p], v_ref[p], lw_ref[0, p], ag_ref[0, p], g_ref[0, p]) for p in ps]
        par = [(kk_ref[p], ka_ref[p], rk_ref[p], lg_ref[p], lb_ref[p]) for p in ps]
        outs, hts = _wkv_group_chunk(rev, tiles, par, [ht_ref[p] for p in ps])
        for q, p in enumerate(ps):
            o_ref[p] = outs[q]
            ht_ref[p] = hts[q]
        return carry

    lax.fori_loop(0, npairs // WKV_GROUP, group, 0)


def wkv(lay, rev, r, k, v, lw, ag, g, params):
    npairs = lay.d // PAIR
    c = WKV_CHUNK
    nch = lay.tt // c
    cch = lay.ctx // c
    dr = 1 if rev else 0

    def rowblk(b, ci):
        if rev:
            return b * nch + jnp.where(ci < cch, cch - 1 - ci, nch + cch - 1 - ci)
        return b * nch + ci

    tok = pl.BlockSpec((npairs, c, PAIR), lambda b, ci: (0, rowblk(b, ci), 0))
    tokd = pl.BlockSpec((1, npairs, c, PAIR), lambda b, ci: (dr, 0, rowblk(b, ci), 0))
    par = pl.BlockSpec((npairs, 1, PAIR), lambda b, ci: (0, 0, 0))
    return pl.pallas_call(
        functools.partial(_wkv_body, rev, npairs),
        grid=(lay.b, nch),
        in_specs=[tok, tok, tok, tokd, tokd, tokd] + [par] * 5,
        out_specs=tok,
        out_shape=jax.ShapeDtypeStruct((npairs, lay.n, PAIR), F32),
        scratch_shapes=[pltpu.VMEM((npairs, PAIR, PAIR), F32)],
        compiler_params=_cparams(("parallel", "arbitrary")),
        name="wkv_rev" if rev else "wkv_fwd",
    )(r, k, v, lw, ag, g, *params)


def _rw_out_body(npairs, o0_ref, o1_ref, h_ref, mod_ref, w_ref, out_ref):
    acc = jnp.concatenate([o0_ref[p] + o1_ref[p] for p in range(npairs)], axis=-1)
    out_ref[...] = h_ref[...] + mod_ref[0, 2:3, :] * _dot(acc, w_ref[...])


def rw_out(lay, o0, o1, h, mod, w):
    d = lay.d
    npairs = d // PAIR
    pm = pl.BlockSpec((npairs, TM, PAIR), lambda i: (0, i, 0))
    return pl.pallas_call(
        functools.partial(_rw_out_body, npairs),
        grid=(lay.nblocks,),
        in_specs=[pm, pm, _rows(d), _modspec(lay), _full((d, d))],
        out_specs=_rows(d),
        out_shape=jax.ShapeDtypeStruct((lay.n, d), F32),
        compiler_params=_cparams(("parallel",)),
        name="rw_out",
    )(o0, o1, h, mod, w)


def rwkv_layer(lay, h, mod, gain, mu, w_r, w_k, w_v, w_o, w0, w1, w2, a0, a1, a2, g1, g2, k_k, k_a, r_k,
               lnx_g, lnx_b):
    npairs = lay.d // PAIR
    xr, xw, xk, xv, xa, xg = rw_prep(lay, h, gain, mod, mu)
    r = linear_pairs(lay, xr, w_r.astype(BF16))
    k = linear_pairs(lay, xk, w_k.astype(BF16))
    v = linear_pairs(lay, xv, w_v.astype(BF16))
    lw, ag, g = rw_lora(lay, xw, xa, xg, w0, w1, w2, a0, a1, a2, g1, g2)
    params = [t.reshape(npairs, 1, PAIR) for t in (k_k, k_a, r_k.reshape(-1), lnx_g, lnx_b)]
    o0 = wkv(lay, False, r, k, v, lw, ag, g, params)
    o1 = wkv(lay, True, r, k, v, lw, ag, g, params)
    return rw_out(lay, o0, o1, h, mod, w_o.astype(BF16))


def rope_tables(lay, rot_dim):
    quarter = rot_dim // 4
    half = rot_dim // 2
    t = np.arange(lay.seq)
    inv = ROPE_THETA ** (-np.arange(quarter, dtype=np.float32) / quarter)
    ang = np.concatenate([(t // GRID_W)[:, None].astype(np.float32) * inv,
                          (t % GRID_W)[:, None].astype(np.float32) * inv], axis=-1)
    ang = jnp.asarray(ang, F32)
    cos, sin = jnp.cos(ang), jnp.sin(ang)
    pad = LANES - rot_dim
    zer = jnp.zeros((lay.seq, half), F32)
    c = jnp.concatenate([cos, cos, jnp.ones((lay.seq, pad), F32)], axis=-1)
    s_lo = jnp.concatenate([-sin, zer, jnp.zeros((lay.seq, pad), F32)], axis=-1)
    s_hi = jnp.concatenate([zer, sin, jnp.zeros((lay.seq, pad), F32)], axis=-1)

    def with_ctx(tab, fill):
        return jnp.concatenate([jnp.full((lay.ctx, LANES), fill, F32), tab], axis=0)

    return with_ctx(c, 1.0), with_ctx(s_lo, 0.0), with_ctx(s_hi, 0.0)


def _rope(x, c, s_lo, s_hi, half):
    return x * c + pltpu.roll(x, LANES - half, 1) * s_lo + pltpu.roll(x, half, 1) * s_hi


def _attn_rows(ctx, rc, row0, q_ref, k, v, o_ref):
    n = q_ref.shape[0] // rc
    ss = []
    for c in range(n):
        s = _dot_nt(q_ref[c * rc:(c + 1) * rc, :], k)
        if row0 is not None and row0 + c * rc < ctx:
            qrow = row0 + c * rc + lax.broadcasted_iota(jnp.int32, s.shape, 0)
            kcol = lax.broadcasted_iota(jnp.int32, s.shape, 1)
            s = jnp.where(jnp.logical_or(qrow >= ctx, kcol < ctx), s, -1e30)
        ss.append(s)
    ms = [jnp.max(s, axis=-1, keepdims=True) for s in ss]
    ps = [jnp.exp(ss[c] - ms[c]).astype(BF16) for c in range(n)]
    dv = v.shape[1]
    v1 = jnp.concatenate([v, jnp.ones_like(v)], axis=1)
    for c in range(n):
        o = jnp.dot(ps[c], v1, preferred_element_type=F32)
        o_ref[c * rc:(c + 1) * rc, :] = (o[:, 0:dv] / o[:, dv:dv + 1]).astype(o_ref.dtype)


def _attn_body(ctx, q_ref, k_ref, v_ref, o_ref):
    k = k_ref[...]
    v = v_ref[...]

    @pl.when(pl.program_id(2) == 0)
    def _():
        _attn_rows(ctx, ATTN_ROWS, 0, q_ref, k, v, o_ref)

    @pl.when(pl.program_id(2) != 0)
    def _():
        _attn_rows(ctx, ATTN_ROWS, None, q_ref, k, v, o_ref)


def attention(lay, q, k, v, n_heads, kv_group, dq, dv, tq=768):
    nq = lay.tt // tq
    assert lay.ctx <= tq and tq % ATTN_ROWS == 0
    k3 = k.reshape(lay.b, lay.tt, k.shape[1])
    v3 = v.reshape(lay.b, lay.tt, v.shape[1])
    return pl.pallas_call(
        functools.partial(_attn_body, lay.ctx),
        grid=(lay.b, n_heads, nq),
        in_specs=[pl.BlockSpec((tq, dq), lambda b, h, i: (b * nq + i, h)),
                  pl.BlockSpec((None, lay.tt, dq), lambda b, h, i: (b, 0, h // kv_group)),
                  pl.BlockSpec((None, lay.tt, dv), lambda b, h, i: (b, 0, h // kv_group))],
        out_specs=pl.BlockSpec((tq, dv), lambda b, h, i: (b * nq + i, h)),
        out_shape=jax.ShapeDtypeStruct((lay.n, n_heads * dv), BF16),
        compiler_params=_cparams(("parallel", "parallel", "arbitrary")),
        name="attention",
    )(q, k3, v3)


def _out_proj_body(x_ref, w_ref, b_ref, h_ref, mod_ref, out_ref):
    y = _dot(x_ref[...], w_ref[...]) + b_ref[...]
    out_ref[...] = h_ref[...] + mod_ref[0, 2:3, :] * y


def out_proj(lay, x, w, bias, h, mod):
    d = lay.d
    kdim = x.shape[1]
    return pl.pallas_call(
        _out_proj_body,
        grid=(lay.nblocks,),
        in_specs=[_rows(kdim), _full((kdim, d)), _full((1, d)), _rows(d), _modspec(lay)],
        out_specs=_rows(d),
        out_shape=jax.ShapeDtypeStruct((lay.n, d), F32),
        compiler_params=_cparams(("parallel",)),
        name="out_proj",
    )(x, w.astype(BF16), bias.reshape(1, d), h, mod)


def _mla_down_body(h_ref, g_ref, mod_ref, w_ref, qan_ref, kvn_ref, krn_ref, c_ref, slo_ref, shi_ref,
                   q_ref, ckv_ref, kr_ref):
    u = _rms(h_ref[...], g_ref[...]) * (1.0 + mod_ref[0, 1:2, :]) + mod_ref[0, 0:1, :]
    down = _dot(u, w_ref[...])
    q_ref[...] = _rms(down[:, 0:MLA_Q_RANK], qan_ref[...]).astype(BF16)
    ckv_ref[...] = _rms(down[:, MLA_Q_RANK:MLA_Q_RANK + MLA_KV_RANK], kvn_ref[...]).astype(BF16)
    kr = down[:, MLA_Q_RANK + MLA_KV_RANK:]
    ms = jnp.sum(kr * kr, axis=-1, keepdims=True) * (1.0 / MLA_ROPE)
    kr = kr * lax.rsqrt(ms + NORM_EPS) * krn_ref[...]
    kr_ref[...] = _rope(kr, c_ref[...], slo_ref[...], shi_ref[...], MLA_ROPE // 2).astype(BF16)


def _mla_q_body(scale, x_ref, w_ref, nn_ref, rn_ref, c_ref, slo_ref, shi_ref, q_ref):
    q = _dot(x_ref[...], w_ref[...])
    hw = 2 * LANES
    for hd in range(MLA_HEADS):
        nope = q[:, hd * hw:hd * hw + LANES]
        rope = q[:, hd * hw + LANES:(hd + 1) * hw]
        nope = _rms(nope, nn_ref[...])
        ms = jnp.sum(rope * rope, axis=-1, keepdims=True) * (1.0 / MLA_ROPE)
        rope = rope * lax.rsqrt(ms + NORM_EPS) * rn_ref[...]
        rope = _rope(rope, c_ref[...], slo_ref[...], shi_ref[...], MLA_ROPE // 2)
        q_ref[:, hd * hw:hd * hw + LANES] = (nope * scale).astype(BF16)
        q_ref[:, hd * hw + LANES:(hd + 1) * hw] = (rope * scale).astype(BF16)


def _mla_kv_body(x_ref, w_ref, kn_ref, kr_ref, k_ref, v_ref):
    kvb = _dot(x_ref[...], w_ref[...])
    hw = 2 * LANES
    kr = kr_ref[...]
    for hd in range(MLA_HEADS):
        k_ref[:, hd * hw:hd * hw + LANES] = _rms(kvb[:, hd * LANES:(hd + 1) * LANES], kn_ref[...]).astype(BF16)
        k_ref[:, hd * hw + LANES:(hd + 1) * hw] = kr
    v_ref[...] = kvb[:, MLA_HEADS * LANES:].astype(BF16)


def mla_layer(lay, h, mod, gain, w_down, qa_norm, w_qb, kv_norm, w_kvb, qn_nope, qn_rope, kn_nope, kn_rope, w_o):
    d = lay.d
    tabs = rope_tables(lay, MLA_ROPE)
    pad64 = lambda t: jnp.concatenate([t, jnp.zeros((LANES - MLA_ROPE,), F32)]).reshape(1, LANES)
    wd = jnp.pad(w_down, ((0, 0), (0, LANES - MLA_ROPE))).astype(BF16)
    nd = wd.shape[1]
    q_lat, ckv, kr = pl.pallas_call(
        _mla_down_body,
        grid=(lay.nblocks,),
        in_specs=[_rows(d), _full((1, d)), _modspec(lay), _full((d, nd)), _full((1, MLA_Q_RANK)),
                  _full((1, MLA_KV_RANK)), _full((1, LANES))] + [_posrows(lay)] * 3,
        out_specs=[_rows(MLA_Q_RANK), _rows(MLA_KV_RANK), _rows(LANES)],
        out_shape=[jax.ShapeDtypeStruct((lay.n, MLA_Q_RANK), BF16), jax.ShapeDtypeStruct((lay.n, MLA_KV_RANK), BF16),
                   jax.ShapeDtypeStruct((lay.n, LANES), BF16)],
        compiler_params=_cparams(("parallel",)),
        name="mla_down",
    )(h, gain.reshape(1, d), mod, wd, qa_norm.reshape(1, -1), kv_norm.reshape(1, -1), pad64(kn_rope), *tabs)

    hq = MLA_NOPE + MLA_ROPE
    wq = w_qb.reshape(MLA_Q_RANK, MLA_HEADS, hq)
    wq = jnp.pad(wq, ((0, 0), (0, 0), (0, 2 * LANES - hq))).reshape(MLA_Q_RANK, MLA_HEADS * 2 * LANES).astype(BF16)
    dq = 2 * LANES
    scale = float(hq) ** -0.5
    q = pl.pallas_call(
        functools.partial(_mla_q_body, scale),
        grid=(lay.nblocks,),
        in_specs=[_rows(MLA_Q_RANK), _full(wq.shape), _full((1, LANES)), _full((1, LANES))] + [_posrows(lay)] * 3,
        out_specs=_rows(MLA_HEADS * dq),
        out_shape=jax.ShapeDtypeStruct((lay.n, MLA_HEADS * dq), BF16),
        compiler_params=_cparams(("parallel",)),
        name="mla_q",
    )(q_lat, wq, qn_nope.reshape(1, LANES), pad64(qn_rope), *tabs)

    wkv_ = w_kvb.reshape(MLA_KV_RANK, MLA_HEADS, MLA_NOPE + MLA_V)
    wkv_ = jnp.concatenate([wkv_[:, :, :MLA_NOPE].reshape(MLA_KV_RANK, -1),
                            wkv_[:, :, MLA_NOPE:].reshape(MLA_KV_RANK, -1)], axis=1).astype(BF16)
    k, v = pl.pallas_call(
        _mla_kv_body,
        grid=(lay.nblocks,),
        in_specs=[_rows(MLA_KV_RANK), _full(wkv_.shape), _full((1, LANES)), _rows(LANES)],
        out_specs=[_rows(MLA_HEADS * dq), _rows(MLA_HEADS * MLA_V)],
        out_shape=[jax.ShapeDtypeStruct((lay.n, MLA_HEADS * dq), BF16),
                   jax.ShapeDtypeStruct((lay.n, MLA_HEADS * MLA_V), BF16)],
        compiler_params=_cparams(("parallel",)),
        name="mla_kv",
    )(ckv, wkv_, kn_nope.reshape(1, LANES), kr)

    o = attention(lay, q, k, v, MLA_HEADS, 1, dq, MLA_V)
    return out_proj(lay, o, w_o, jnp.zeros((d,), F32), h, mod)


def _dft_cos_sin(n):
    j = np.arange(n, dtype=np.int64)
    m = (j[:, None] * j[None, :]) % n
    ang = 2.0 * np.pi * m.astype(np.float64) / n
    return np.cos(ang), np.sin(ang)


def _fn_stage1_body(gw, h_ref, g_ref, mod_ref, cs_ref, zc_ref, zs_ref):
    u = (_rms(h_ref[...], g_ref[...]) * (1.0 + mod_ref[0, 1:2, :]) + mod_ref[0, 0:1, :]).astype(BF16)
    for gi in range(FN_GROUPS):
        z = jnp.dot(u[:, gi * gw:(gi + 1) * gw], cs_ref[...], preferred_element_type=F32)
        zc_ref[:, gi * gw:(gi + 1) * gw] = z[:, 0:gw].astype(BF16)
        zs_ref[:, gi * gw:(gi + 1) * gw] = z[:, gw:2 * gw].astype(BF16)


def _fn_stage2_body(lay, ct_ref, st_ref, zc_ref, zs_ref, o_ref):
    j = pl.program_id(1)

    def mix(lo, hi):
        f = (jnp.dot(ct_ref[:, lo:hi], zc_ref[lo:hi, :], preferred_element_type=F32)
             - jnp.dot(st_ref[:, lo:hi], zs_ref[lo:hi, :], preferred_element_type=F32))
        o_ref[...] = f.astype(BF16)

    @pl.when(j < lay.cb)
    def _():
        mix(0, lay.ctx)

    @pl.when(j >= lay.cb)
    def _():
        mix(lay.ctx, lay.tt)


def fourier_layer(lay, h, mod, gain, w, b):
    d = lay.d
    gw = d // FN_GROUPS
    cw, sw = _dft_cos_sin(gw)
    csw = jnp.asarray(np.concatenate([cw, sw], axis=1), F32).astype(BF16)
    zc, zs = pl.pallas_call(
        functools.partial(_fn_stage1_body, gw),
        grid=(lay.nblocks,),
        in_specs=[_rows(d), _full((1, d)), _modspec(lay), _full((gw, 2 * gw))],
        out_specs=[_rows(d), _rows(d)],
        out_shape=[jax.ShapeDtypeStruct((lay.n, d), BF16)] * 2,
        compiler_params=_cparams(("parallel",)),
        name="fn_stage1",
    )(h, gain.reshape(1, d), mod, csw)

    ct = np.zeros((lay.tt, lay.tt), np.float64)
    st = np.zeros((lay.tt, lay.tt), np.float64)
    for lo, n in ((0, lay.ctx), (lay.ctx, lay.seq)):
        c, s = _dft_cos_sin(n)
        sc = 1.0 / math.sqrt(n * gw)
        ct[lo:lo + n, lo:lo + n] = c * sc
        st[lo:lo + n, lo:lo + n] = s * sc
    ct = jnp.asarray(ct, F32).astype(BF16)
    st = jnp.asarray(st, F32).astype(BF16)
    zc3 = zc.reshape(lay.b, lay.tt, d)
    zs3 = zs.reshape(lay.b, lay.tt, d)
    f = pl.pallas_call(
        functools.partial(_fn_stage2_body, lay),
        grid=(lay.b, lay.nb),
        in_specs=[pl.BlockSpec((TM, lay.tt), lambda bb, j: (j, 0)),
                  pl.BlockSpec((TM, lay.tt), lambda bb, j: (j, 0)),
                  pl.BlockSpec((None, lay.tt, d), lambda bb, j: (bb, 0, 0)),
                  pl.BlockSpec((None, lay.tt, d), lambda bb, j: (bb, 0, 0))],
        out_specs=pl.BlockSpec((TM, d), lambda bb, j: (bb * lay.nb + j, 0)),
        out_shape=jax.ShapeDtypeStruct((lay.n, d), BF16),
        compiler_params=_cparams(("parallel", "arbitrary")),
        name="fn_stage2",
    )(ct, st, zc3, zs3)
    return out_proj(lay, f, w, b, h, mod)


def _gqa_qkv_body(scale, h_ref, g_ref, mod_ref, w_ref, qn_ref, kn_ref, c_ref, slo_ref, shi_ref,
                  q_ref, k_ref, v_ref):
    u = _rms(h_ref[...], g_ref[...]) * (1.0 + mod_ref[0, 1:2, :]) + mod_ref[0, 0:1, :]
    qkv = _dot(u, w_ref[...])
    c, slo, shi = c_ref[...], slo_ref[...], shi_ref[...]
    n_q = GQA_HEADS * GQA_HEAD
    n_kv = GQA_KV_HEADS * GQA_HEAD
    for hd in range(GQA_HEADS):
        x = _rms(qkv[:, hd * LANES:(hd + 1) * LANES], qn_ref[...])
        q_ref[:, hd * LANES:(hd + 1) * LANES] = (_rope(x, c, slo, shi, GQA_HEAD // 2) * scale).astype(BF16)
    for hd in range(GQA_KV_HEADS):
        x = _rms(qkv[:, n_q + hd * LANES:n_q + (hd + 1) * LANES], kn_ref[...])
        k_ref[:, hd * LANES:(hd + 1) * LANES] = _rope(x, c, slo, shi, GQA_HEAD // 2).astype(BF16)
    v_ref[...] = qkv[:, n_q + n_kv:].astype(BF16)


def gqa_layer(lay, h, mod, gain, w_qkv, q_norm, k_norm, w_o):
    d = lay.d
    tabs = rope_tables(lay, GQA_HEAD)
    n_q = GQA_HEADS * GQA_HEAD
    n_kv = GQA_KV_HEADS * GQA_HEAD
    q, k, v = pl.pallas_call(
        functools.partial(_gqa_qkv_body, float(GQA_HEAD) ** -0.5),
        grid=(lay.nblocks,),
        in_specs=[_rows(d), _full((1, d)), _modspec(lay), _full((d, n_q + 2 * n_kv)), _full((1, LANES)),
                  _full((1, LANES))] + [_posrows(lay)] * 3,
        out_specs=[_rows(n_q), _rows(n_kv), _rows(n_kv)],
        out_shape=[jax.ShapeDtypeStruct((lay.n, n_q), BF16), jax.ShapeDtypeStruct((lay.n, n_kv), BF16),
                   jax.ShapeDtypeStruct((lay.n, n_kv), BF16)],
        compiler_params=_cparams(("parallel",)),
        name="gqa_qkv",
    )(h, gain.reshape(1, d), mod, w_qkv.astype(BF16), q_norm.reshape(1, LANES), k_norm.reshape(1, LANES), *tabs)
    o = attention(lay, q, k, v, GQA_HEADS, GQA_HEADS // GQA_KV_HEADS, GQA_HEAD, GQA_HEAD)
    return out_proj(lay, o, w_o, jnp.zeros((d,), F32), h, mod)


def _row_tiles(d):
    return d // LANES


def _store_row_tiles(ref, x):
    rows, d = x.shape
    nt = _row_tiles(d)
    for j in range(nt):
        ref[pl.ds(j, rows, stride=nt), :] = x[:, j * LANES:(j + 1) * LANES]


def _load_row_tiles(ref, rows, d):
    nt = _row_tiles(d)
    return jnp.concatenate([ref[pl.ds(j, rows, stride=nt), :] for j in range(nt)], axis=-1)


def _pack_halves(x):
    half = x.shape[1] // 2
    hi = pltpu.bitcast(x[:, :half].astype(BF16).astype(F32), jnp.uint32)
    lo = pltpu.bitcast(x[:, half:].astype(BF16).astype(F32), jnp.uint32)
    return hi | (lo >> 16)


def _unpack_halves(u):
    hi = pltpu.bitcast(u & jnp.uint32(0xFFFF0000), F32)
    lo = pltpu.bitcast(u << 16, F32)
    return jnp.concatenate([hi, lo], axis=-1)


def _ffn_prep_body(h_ref, g_ref, mod_ref, wr_ref, br_ref, tri_ref, v_ref, idx_ref, wt_ref, rank_ref, cnt_ref,
                   carry_ref):
    @pl.when(pl.program_id(0) == 0)
    def _():
        carry_ref[...] = jnp.zeros_like(carry_ref)

    v = _rms(h_ref[...], g_ref[...]) * (1.0 + mod_ref[0, 4:5, :]) + mod_ref[0, 3:4, :]
    _store_row_tiles(v_ref, _pack_halves(v))
    logit = _dot(v, wr_ref[...]) + br_ref[...]
    lane = lax.broadcasted_iota(jnp.int32, logit.shape, 1).astype(F32)
    idx = jnp.zeros_like(logit)
    ex = jnp.zeros_like(logit)
    den = 0.0
    m0 = None
    hot = []
    for kx in range(TOP_K):
        m = jnp.max(logit, axis=-1, keepdims=True)
        am = jnp.min(jnp.where(logit == m, lane, float(LANES)), axis=-1, keepdims=True)
        if kx == 0:
            m0 = m
        e = jnp.exp(m - m0)
        den = den + e
        idx = jnp.where(lane == kx, am, idx)
        ex = jnp.where(lane == kx, e, ex)
        hot.append(lane == am)
        logit = jnp.where(hot[kx], -jnp.inf, logit)
    idx_ref[...] = idx.astype(jnp.int32)
    wt_ref[...] = ex / den
    chosen = jnp.where(hot[0] | hot[1] | hot[2] | hot[3], 1.0, 0.0)
    before = carry_ref[...] + jnp.dot(tri_ref[...], chosen.astype(BF16), preferred_element_type=F32)
    rank = jnp.zeros_like(logit)
    for kx in range(TOP_K):
        rk = jnp.sum(jnp.where(hot[kx], before, 0.0), axis=-1, keepdims=True)
        rank = jnp.where(lane == kx, rk, rank)
    rank_ref[...] = rank.astype(jnp.int32)
    carry_ref[...] = carry_ref[...] + jnp.sum(chosen, axis=0, keepdims=True)
    cnt_ref[...] = carry_ref[...]


def ffn_prep(lay, h, gain, mod, w_r, b_r):
    d = lay.d
    ne = w_r.shape[1]
    wr = jnp.pad(w_r, ((0, 0), (0, LANES - ne))).astype(BF16)
    br = jnp.concatenate([b_r, jnp.full((LANES - ne,), -1e30, F32)]).reshape(1, LANES)
    tri = jnp.asarray(np.tril(np.ones((TM, TM), np.float32), -1)).astype(BF16)
    return pl.pallas_call(
        _ffn_prep_body,
        grid=(lay.nblocks,),
        in_specs=[_rows(d), _full((1, d)), _modspec(lay), _full((d, LANES)), _full((1, LANES)), _full((TM, TM))],
        out_specs=[_rows(LANES, TM * _row_tiles(d // 2)), _rows(LANES), _rows(LANES), _rows(LANES), _full((1, LANES))],
        out_shape=[jax.ShapeDtypeStruct((lay.n * _row_tiles(d // 2), LANES), jnp.uint32),
                   jax.ShapeDtypeStruct((lay.n, LANES), jnp.int32),
                   jax.ShapeDtypeStruct((lay.n, LANES), F32), jax.ShapeDtypeStruct((lay.n, LANES), jnp.int32),
                   jax.ShapeDtypeStruct((1, LANES), F32)],
        scratch_shapes=[pltpu.VMEM((1, LANES), F32)],
        compiler_params=_cparams(("arbitrary",)),
        name="ffn_prep",
    )(h, gain.reshape(1, d), mod, wr, br, tri)


def _row_at(ref, row, nt):
    return ref.at[pl.ds(pl.multiple_of(row * nt, nt), nt)]


def _dispatch_body(nt, ne, pad_ref, dest_ref, v_ref, xs_ref, zero_ref, sem):
    @pl.when(pl.program_id(0) == 0)
    def _():
        zero_ref[...] = jnp.zeros_like(zero_ref)
        for e in range(ne):
            def issue_zero(i, carry, e=e):
                pltpu.make_async_copy(zero_ref, _row_at(xs_ref, pad_ref[0, e] + i, nt), sem).start()
                return carry

            lax.fori_loop(0, pad_ref[1, e], issue_zero, 0)
        for e in range(ne):
            def drain_zero(i, carry):
                pltpu.make_async_copy(zero_ref, _row_at(xs_ref, 0, nt), sem).wait()
                return carry

            lax.fori_loop(0, pad_ref[1, e], drain_zero, 0)

    def issue(t, carry):
        for kx in range(TOP_K):
            pltpu.make_async_copy(_row_at(v_ref, t, nt), _row_at(xs_ref, dest_ref[0, 0, t * TOP_K + kx], nt),
                                  sem).start(priority=kx % 2)
        return carry

    lax.fori_loop(0, TM, issue, 0)

    def drain(t, carry):
        for kx in range(TOP_K):
            pltpu.make_async_copy(_row_at(v_ref, 0, nt), _row_at(xs_ref, 0, nt), sem).wait()
        return carry

    lax.fori_loop(0, TM, drain, 0)


def dispatch(lay, v, dest3, pad_info, n_rows):
    nt = _row_tiles(lay.d // 2)
    ne = pad_info.shape[1]
    grid_spec = pltpu.PrefetchScalarGridSpec(
        num_scalar_prefetch=1,
        grid=(lay.nblocks,),
        in_specs=[pl.BlockSpec((1, 1, TM * TOP_K), lambda i, pad: (i, 0, 0), memory_space=pltpu.SMEM),
                  pl.BlockSpec((TM * nt, LANES), lambda i, pad: (i, 0))],
        out_specs=pl.BlockSpec(memory_space=pl.ANY),
        scratch_shapes=[pltpu.VMEM((nt, LANES), v.dtype), pltpu.SemaphoreType.DMA],
    )
    return pl.pallas_call(
        functools.partial(_dispatch_body, nt, ne),
        grid_spec=grid_spec,
        out_shape=jax.ShapeDtypeStruct((n_rows * nt, LANES), v.dtype),
        compiler_params=_cparams(("arbitrary",)),
        name="moe_dispatch",
    )(pad_info, dest3, v)


def _experts_body(layer, ff, be_ref, bv_ref, first_ref, next_ref, x_ref, bg_ref, bu_ref, bd_ref, wg_hbm, wu_hbm, wd_hbm,
                  y_ref, stage_gu, stage_d, work_gu, work_d, sem):
    i = pl.program_id(0)
    d = 2 * (x_ref.shape[0] // EXPERT_ROWS) * LANES

    def copies(e):
        return (pltpu.make_async_copy(wg_hbm.at[layer, e], stage_gu.at[:, pl.ds(0, ff)], sem),
                pltpu.make_async_copy(wu_hbm.at[layer, e], stage_gu.at[:, pl.ds(ff, ff)], sem),
                pltpu.make_async_copy(wd_hbm.at[layer, e], stage_d, sem))

    @pl.when(i == 0)
    def _():
        for cp in copies(be_ref[0]):
            cp.start()

    @pl.when(first_ref[i] > 0)
    def _():
        for cp in copies(be_ref[i]):
            cp.wait()
        rows = 64

        def cast_gu(c, carry):
            r0 = pl.multiple_of(c * rows, rows)
            work_gu[pl.ds(r0, rows), :] = stage_gu[pl.ds(r0, rows), :].astype(BF16)
            return carry

        lax.fori_loop(0, d // rows, cast_gu, 0)

        def cast_d(c, carry):
            r0 = pl.multiple_of(c * rows, rows)
            work_d[pl.ds(r0, rows), :] = stage_d[pl.ds(r0, rows), :].astype(BF16)
            return carry

        lax.fori_loop(0, ff // rows, cast_d, 0)

        @pl.when(next_ref[i] >= 0)
        def _():
            for cp in copies(next_ref[i]):
                cp.start()

    @pl.when(bv_ref[i] > 0)
    def _():
        x = _unpack_halves(_load_row_tiles(x_ref, EXPERT_ROWS, d // 2)).astype(BF16)
        hgu = jnp.dot(x, work_gu[...], preferred_element_type=F32)
        gt = jnp.minimum(hgu[:, 0:ff] + bg_ref[0], SWIGLU_LIMIT)
        up = jnp.clip(hgu[:, ff:2 * ff] + bu_ref[0], -SWIGLU_LIMIT, SWIGLU_LIMIT)
        act = gt * _sigmoid(SWIGLU_ALPHA * gt) * (up + 1.0)
        _store_row_tiles(y_ref, _dot(act, work_d[...]) + bd_ref[0])

    @pl.when(bv_ref[i] == 0)
    def _():
        y_ref[...] = jnp.zeros_like(y_ref)


def experts(layer, xs, blk_e, blk_valid, blk_first, blk_next, wg, bg, wu, bu, wd, bd):
    nl, ne, d, ff = wg.shape
    nt = _row_tiles(d)
    ntx = _row_tiles(d // 2)
    n_rows = xs.shape[0] // ntx
    r = EXPERT_ROWS
    hbm = pl.BlockSpec(memory_space=pl.ANY)
    grid_spec = pltpu.PrefetchScalarGridSpec(
        num_scalar_prefetch=4,
        grid=(n_rows // r,),
        in_specs=[pl.BlockSpec((r * ntx, LANES), lambda i, be, bv, bf, bn: (jnp.where(bv[i] > 0, i, 0), 0)),
                  pl.BlockSpec((None, 1, 1, ff), lambda i, be, bv, bf, bn: (layer, be[i], 0, 0)),
                  pl.BlockSpec((None, 1, 1, ff), lambda i, be, bv, bf, bn: (layer, be[i], 0, 0)),
                  pl.BlockSpec((None, 1, 1, d), lambda i, be, bv, bf, bn: (layer, be[i], 0, 0)),
                  hbm, hbm, hbm],
        out_specs=pl.BlockSpec((r * nt, LANES), lambda i, be, bv, bf, bn: (i, 0)),
        scratch_shapes=[pltpu.VMEM((d, 2 * ff), F32), pltpu.VMEM((ff, d), F32),
                        pltpu.VMEM((d, 2 * ff), BF16), pltpu.VMEM((ff, d), BF16), pltpu.SemaphoreType.DMA],
    )
    return pl.pallas_call(
        functools.partial(_experts_body, layer, ff),
        grid_spec=grid_spec,
        out_shape=jax.ShapeDtypeStruct((n_rows * nt, LANES), F32),
        compiler_params=_cparams(("arbitrary",)),
        name="experts",
    )(blk_e, blk_valid, blk_first, blk_next, xs, bg.reshape(nl, ne, 1, ff), bu.reshape(nl, ne, 1, ff),
      bd.reshape(nl, ne, 1, d), wg, wu, wd)


def _combine_body(nt, nblocks, dest_ref, dnext_ref, wt_ref, h_ref, mod_ref, y_ref, out_ref, ybuf, fbuf, sems):
    i = pl.program_id(0)
    slot = i % 2

    def gather(dref, sl):
        def issue(t, carry):
            for kx in range(TOP_K):
                pltpu.make_async_copy(_row_at(y_ref, dref[0, 0, t * TOP_K + kx], nt), _row_at(ybuf.at[sl, kx], t, nt),
                                      sems.at[sl]).start(priority=kx % 2)
            return carry

        lax.fori_loop(0, TM, issue, 0)

    @pl.when(i == 0)
    def _():
        gather(dest_ref, 0)

    @pl.when(i + 1 < nblocks)
    def _():
        gather(dnext_ref, 1 - slot)

    def drain(t, carry):
        for kx in range(TOP_K):
            pltpu.make_async_copy(_row_at(y_ref, 0, nt), _row_at(ybuf.at[slot, kx], 0, nt), sems.at[slot]).wait()
        return carry

    lax.fori_loop(0, TM, drain, 0)

    def weigh(t, carry):
        base = pl.multiple_of(t * nt, nt)
        acc = wt_ref[0, 0, t * TOP_K] * ybuf[slot, 0, pl.ds(base, nt), :]
        for kx in range(1, TOP_K):
            acc = acc + wt_ref[0, 0, t * TOP_K + kx] * ybuf[slot, kx, pl.ds(base, nt), :]
        fbuf[pl.ds(base, nt), :] = acc
        return carry

    lax.fori_loop(0, TM, weigh, 0, unroll=4)
    out_ref[...] = h_ref[...] + mod_ref[0, 5:6, :] * _load_row_tiles(fbuf, TM, h_ref.shape[1])


def combine(lay, h, y, dest3, wts3, mod):
    d = lay.d
    nt = _row_tiles(d)
    nb = lay.nblocks
    smem = lambda f: pl.BlockSpec((1, 1, TM * TOP_K), f, memory_space=pltpu.SMEM)
    return pl.pallas_call(
        functools.partial(_combine_body, nt, nb),
        grid=(nb,),
        in_specs=[smem(lambda i: (i, 0, 0)), smem(lambda i: (jnp.minimum(i + 1, nb - 1), 0, 0)),
                  smem(lambda i: (i, 0, 0)), _rows(d), _modspec(lay), pl.BlockSpec(memory_space=pl.ANY)],
        out_specs=_rows(d),
        out_shape=jax.ShapeDtypeStruct((lay.n, d), F32),
        scratch_shapes=[pltpu.VMEM((2, TOP_K, TM * nt, LANES), F32), pltpu.VMEM((TM * nt, LANES), F32),
                        pltpu.SemaphoreType.DMA((2,))],
        compiler_params=_cparams(("arbitrary",)),
        name="moe_combine",
    )(dest3, dest3, wts3, h, mod, y)


def moe_layer(lay, layer, h, mod, gain, w_r, b_r, wg, bg, wu, bu, wd, bd):
    ne = w_r.shape[1]
    r = EXPERT_ROWS
    v, idx, wts, rank, cnt = ffn_prep(lay, h, gain, mod, w_r, b_r)
    n_slot = lay.n * TOP_K
    counts = cnt[0, :ne].astype(jnp.int32)
    padded = (counts + r - 1) // r * r
    pad_end = jnp.cumsum(padded)
    pad_start = pad_end - padded
    dest = pad_start[idx[:, :TOP_K]] + rank[:, :TOP_K]
    dest3 = dest.reshape(lay.nblocks, 1, TM * TOP_K)
    n_rows = n_slot + ne * r
    blk_start = jnp.arange(n_rows // r, dtype=jnp.int32) * r
    blk_e = jnp.minimum(jnp.sum((blk_start[:, None] >= pad_end[None, :]).astype(jnp.int32), axis=1), ne - 1)
    blk_valid = (blk_start < pad_end[-1]).astype(jnp.int32)
    pad_info = jnp.stack([jnp.concatenate([pad_start + counts, pad_end[-1:]]),
                          jnp.concatenate([padded - counts, n_rows - pad_end[-1:]])]).astype(jnp.int32)
    xs = dispatch(lay, v, dest3, pad_info, n_rows)
    prev_e = jnp.concatenate([jnp.full((1,), -1, jnp.int32), blk_e[:-1]])
    blk_first = ((blk_e != prev_e) & (blk_valid > 0)).astype(jnp.int32)
    eid = jnp.arange(ne, dtype=jnp.int32)
    later = (eid[None, :] > eid[:, None]) & (padded[None, :] > 0)
    nxt = jnp.min(jnp.where(later, eid[None, :], ne), axis=1)
    blk_next = jnp.where(nxt[blk_e] < ne, nxt[blk_e], -1).astype(jnp.int32)
    y = experts(layer, xs, blk_e, blk_valid, blk_first, blk_next, wg, bg, wu, bu, wd, bd)
    wts3 = wts[:, :TOP_K].reshape(lay.nblocks, 1, TM * TOP_K)
    return combine(lay, h, y, dest3, wts3, mod)


def kernel(x, c, ctx, c_ctx, mod_w, mod_b, norm_mix, norm_ffn, router_w, router_b, exp_w_gate, exp_b_gate, exp_w_up, exp_b_up, exp_w_down, exp_b_down, rw_mu, rw_wr, rw_wk, rw_wv, rw_wo, rw_w0, rw_w1, rw_w2, rw_a0, rw_a1, rw_a2, rw_g1, rw_g2, rw_kk, rw_ka, rw_rk, rw_lnx_g, rw_lnx_b, mla_w_down, mla_qa_norm, mla_w_qb, mla_kv_norm, mla_w_kvb, mla_qn_nope, mla_qn_rope, mla_kn_nope, mla_kn_rope, mla_wo, fn_w, fn_b, gqa_w_qkv, gqa_q_norm, gqa_k_norm, gqa_wo):
    bsz, seq, d = x.shape
    ctx_len = ctx.shape[1]
    depth = mod_w.shape[0]
    lay = Layout(bsz, ctx_len, seq, d)
    assert bsz < 16 and d // PAIR * PAIR == d

    cin = jnp.zeros((16, d), F32).at[:bsz].set(c).at[bsz].set(c_ctx)
    mod_all = modulation(cin, mod_w, mod_b).reshape(depth, 16, 6, d)

    h = jnp.concatenate([ctx, x], axis=1).reshape(lay.n, d)
    n_mixers = 4
    for i in range(depth):
        m, j = i % n_mixers, i // n_mixers
        mod = mod_all[i]
        if m == 0:
            h = rwkv_layer(lay, h, mod, norm_mix[i], rw_mu[j], rw_wr[j], rw_wk[j], rw_wv[j], rw_wo[j], rw_w0[j],
                           rw_w1[j], rw_w2[j], rw_a0[j], rw_a1[j], rw_a2[j], rw_g1[j], rw_g2[j], rw_kk[j],
                           rw_ka[j], rw_rk[j], rw_lnx_g[j], rw_lnx_b[j])
        elif m == 1:
            h = mla_layer(lay, h, mod, norm_mix[i], mla_w_down[j], mla_qa_norm[j], mla_w_qb[j], mla_kv_norm[j],
                          mla_w_kvb[j], mla_qn_nope[j], mla_qn_rope[j], mla_kn_nope[j], mla_kn_rope[j], mla_wo[j])
        elif m == 2:
            h = fourier_layer(lay, h, mod, norm_mix[i], fn_w[j], fn_b[j])
        else:
            h = gqa_layer(lay, h, mod, norm_mix[i], gqa_w_qkv[j], gqa_q_norm[j], gqa_k_norm[j], gqa_wo[j])
        h = moe_layer(lay, i, h, mod, norm_ffn[i], router_w[i], router_b[i], exp_w_gate, exp_b_gate, exp_w_up, exp_b_up,
                      exp_w_down, exp_b_down)
    return h.reshape(bsz, lay.tt, d)[:, ctx_len:, :]
```
